```python
import math
import jax, jax.numpy as jnp
from jax import lax
import numpy as np

D_MODEL = 1024
BATCH = 4
SEQ = 8192
DEPTH = 1

D_SSM = D_MODEL // 2
D_CONV = D_MODEL - D_SSM
SSM_GROUP = 16
N_SSM_GROUPS = D_SSM // SSM_GROUP
SSM_STATE = 64
CONV_WIDTH = 31
D_IN = D_SSM + 2 * D_CONV
N_EXPERTS = 32
TOP_K = 4
D_FF = D_MODEL
SWIGLU_ALPHA = 1.702
SWIGLU_LIMIT = 7.0
ROW_BLOCK = 128
DT_MIN = 1e-3
DT_MAX = 1e-1
RMS_EPS = 1e-6
LN_EPS = 1e-5

kernel_name = "hybrid_s5_conformer_moe_adaln"


def rmsnorm(x, g, eps=RMS_EPS):
    xf = x.astype(jnp.float32)
    y = xf * lax.rsqrt(jnp.mean(xf * xf, axis=-1, keepdims=True) + eps)
    return (y * g.astype(jnp.float32)).astype(x.dtype)


def layernorm(x, g, b, eps=LN_EPS):
    xf = x.astype(jnp.float32)
    mu = jnp.mean(xf, axis=-1, keepdims=True)
    xc = xf - mu
    y = xc * lax.rsqrt(jnp.mean(xc * xc, axis=-1, keepdims=True) + eps)
    return (y * g.astype(jnp.float32) + b.astype(jnp.float32)).astype(x.dtype)


def s5_mixer(u, lam_re, lam_im, log_dt, b_re, b_im, c_re, c_im, d_skip, w_glu, b_glu):
    bsz, seq, _ = u.shape
    dtype = u.dtype
    f32 = jnp.float32
    ug = u.astype(f32).reshape(bsz, seq, N_SSM_GROUPS, SSM_GROUP)
    lr = lam_re.astype(f32)
    li = lam_im.astype(f32)
    dt = jnp.exp(log_dt.astype(f32))[:, None]
    mag = jnp.exp(lr * dt)
    ab_re = mag * jnp.cos(li * dt)
    ab_im = mag * jnp.sin(li * dt)
    den = lr * lr + li * li
    nr = ab_re - 1.0
    q_re = (nr * lr + ab_im * li) / den
    q_im = (ab_im * lr - nr * li) / den
    br = b_re.astype(f32)
    bi = b_im.astype(f32)
    bb_re = q_re[..., None] * br - q_im[..., None] * bi
    bb_im = q_re[..., None] * bi + q_im[..., None] * br
    bu_re = jnp.einsum('blgh,gph->blgp', ug, bb_re)
    bu_im = jnp.einsum('blgh,gph->blgp', ug, bb_im)
    a_re = jnp.broadcast_to(ab_re, bu_re.shape)
    a_im = jnp.broadcast_to(ab_im, bu_im.shape)

    def combine(e1, e2):
        a1r, a1i, b1r, b1i = e1
        a2r, a2i, b2r, b2i = e2
        return (a1r * a2r - a1i * a2i,
                a1r * a2i + a1i * a2r,
                a2r * b1r - a2i * b1i + b2r,
                a2r * b1i + a2i * b1r + b2i)

    _, _, xr, xi = lax.associative_scan(combine, (a_re, a_im, bu_re, bu_im), axis=1)
    y = (jnp.einsum('blgp,ghp->blgh', xr, c_re.astype(f32))
         - jnp.einsum('blgp,ghp->blgh', xi, c_im.astype(f32)))
    y = y + d_skip.astype(f32).reshape(N_SSM_GROUPS, SSM_GROUP) * ug
    y = jax.nn.gelu(y.reshape(bsz, seq, D_SSM)).astype(dtype)
    return y * jax.nn.sigmoid(y @ w_glu + b_glu)


def conformer_conv(v, g, conv_w, conv_b, ln_g, ln_b):
    z = v * jax.nn.sigmoid(g)
    kern = conv_w[:, None, :].astype(z.dtype)
    z = lax.conv_general_dilated(z, kern, window_strides=(1,),
                                 padding=[(CONV_WIDTH - 1, 0)],
                                 dimension_numbers=('NWC', 'WIO', 'NWC'),
                                 feature_group_count=D_CONV) + conv_b
    return jax.nn.silu(layernorm(z, ln_g, ln_b))


def moe_ffn(h, router_w, router_b, w_gate, b_gate, w_up, b_up, w_down, b_down):
    bsz, seq, d = h.shape
    n_tok = bsz * seq
    hf = h.reshape(n_tok, d)
    logits = (hf @ router_w + router_b).astype(jnp.float32)
    top_val, top_idx = lax.top_k(logits, TOP_K)
    top_w = jax.nn.softmax(top_val, axis=-1).astype(h.dtype)
    n_assign = n_tok * TOP_K
    n_blocks = -(-n_assign // ROW_BLOCK) + N_EXPERTS
    n_rows = n_blocks * ROW_BLOCK
    flat_e = top_idx.reshape(-1)
    flat_tok = jnp.repeat(jnp.arange(n_tok, dtype=jnp.int32), TOP_K)
    flat_w = top_w.reshape(-1)
    order = jnp.argsort(flat_e)
    se = flat_e[order]
    counts = jnp.bincount(flat_e, length=N_EXPERTS)
    padded = (counts + ROW_BLOCK - 1) // ROW_BLOCK * ROW_BLOCK
    start = jnp.cumsum(counts) - counts
    pend = jnp.cumsum(padded)
    pstart = pend - padded
    dest = pstart[se] + (jnp.arange(n_assign, dtype=jnp.int32) - start[se])
    row_tok = jnp.full((n_rows,), n_tok, jnp.int32).at[dest].set(flat_tok[order])
    row_w = jnp.zeros((n_rows,), h.dtype).at[dest].set(flat_w[order])
    block_e = jnp.minimum(
        jnp.searchsorted(pend, jnp.arange(n_blocks, dtype=jnp.int32) * ROW_BLOCK, side='right'),
        N_EXPERTS - 1)
    h_pad = jnp.concatenate([hf, jnp.zeros((1, d), hf.dtype)], axis=0)
    xs = h_pad[row_tok].reshape(n_blocks, ROW_BLOCK, d)

    def expert_block(args):
        xb, e = args
        gate = jnp.minimum(xb @ w_gate[e] + b_gate[e], SWIGLU_LIMIT)
        up = jnp.clip(xb @ w_up[e] + b_up[e], -SWIGLU_LIMIT, SWIGLU_LIMIT)
        glu = gate * jax.nn.sigmoid(SWIGLU_ALPHA * gate)
        return ((up + 1.0) * glu) @ w_down[e] + b_down[e]

    ys = lax.map(expert_block, (xs, block_e)).reshape(n_rows, d)
    out = jax.ops.segment_sum(ys * row_w[:, None], row_tok, num_segments=n_tok + 1)[:n_tok]
    return out.reshape(bsz, seq, d)


def setup_inputs(seed: int = 0) -> dict:
    key = jax.random.key(seed)
    ks = jax.random.split(key, 32)
    f32 = jnp.float32

    def nrm(k, shape, scale):
        return jax.random.normal(k, shape, f32) * scale

    L, D, G, P, H = DEPTH, D_MODEL, N_SSM_GROUPS, SSM_STATE, SSM_GROUP
    E, F = N_EXPERTS, D_FF
    lam_im_base = jnp.broadcast_to(math.pi * jnp.arange(P, dtype=f32), (L, G, P))
    return {
        "x": nrm(ks[0], (BATCH, SEQ, D), 1.0),
        "c": nrm(ks[1], (BATCH, D), 1.0),
        "ada_w": nrm(ks[2], (L, D, 6 * D), 0.5 * D ** -0.5),
        "ada_b": nrm(ks[3], (L, 6 * D), 0.01),
        "norm1_g": 1.0 + nrm(ks[4], (L, D), 0.02),
        "w_in": nrm(ks[5], (L, D, D_IN), D ** -0.5),
        "lam_re": -0.5 + nrm(ks[6], (L, G, P), 0.01),
        "lam_im": lam_im_base + nrm(ks[7], (L, G, P), 0.01),
        "log_dt": jax.random.uniform(ks[8], (L, G), f32, math.log(DT_MIN), math.log(DT_MAX)),
        "b_re": nrm(ks[9], (L, G, P, H), (2 * H) ** -0.5),
        "b_im": nrm(ks[10], (L, G, P, H), (2 * H) ** -0.5),
        "c_re": nrm(ks[11], (L, G, H, P), (2 * P) ** -0.5),
        "c_im": nrm(ks[12], (L, G, H, P), (2 * P) ** -0.5),
        "d_skip": nrm(ks[13], (L, D_SSM), 1.0),
        "w_glu": nrm(ks[14], (L, D_SSM, D_SSM), D_SSM ** -0.5),
        "b_glu": nrm(ks[15], (L, D_SSM), 0.01),
        "conv_w": nrm(ks[16], (L, CONV_WIDTH, D_CONV), CONV_WIDTH ** -0.5),
        "conv_b": nrm(ks[17], (L, D_CONV), 0.01),
        "ln_g": 1.0 + nrm(ks[18], (L, D_CONV), 0.02),
        "ln_b": nrm(ks[19], (L, D_CONV), 0.01),
        "out_norm_g": 1.0 + nrm(ks[20], (L, D), 0.02),
        "w_out": nrm(ks[21], (L, D, D), D ** -0.5),
        "norm2_g": 1.0 + nrm(ks[22], (L, D), 0.02),
        "router_w": nrm(ks[23], (L, D, E), D ** -0.5),
        "router_b": nrm(ks[24], (L, E), 0.01),
        "w_gate": nrm(ks[25], (L, E, D, F), D ** -0.5),
        "b_gate": nrm(ks[26], (L, E, F), 0.01),
        "w_up": nrm(ks[27], (L, E, D, F), D ** -0.5),
        "b_up": nrm(ks[28], (L, E, F), 0.01),
        "w_down": nrm(ks[29], (L, E, F, D), F ** -0.5),
        "b_down": nrm(ks[30], (L, E, D), 0.01),
        "final_g": 1.0 + nrm(ks[31], (D,), 0.02),
    }


def reference(x, c, ada_w, ada_b, norm1_g, w_in, lam_re, lam_im, log_dt, b_re, b_im,
              c_re, c_im, d_skip, w_glu, b_glu, conv_w, conv_b, ln_g, ln_b, out_norm_g,
              w_out, norm2_g, router_w, router_b, w_gate, b_gate, w_up, b_up, w_down,
              b_down, final_g):
    cond = jax.nn.silu(c)
    for i in range(DEPTH):
        mod = cond @ ada_w[i] + ada_b[i]
        sh1, sc1, g1, sh2, sc2, g2 = [m[:, None, :] for m in jnp.split(mod, 6, axis=-1)]
        h = rmsnorm(x, norm1_g[i]) * (1.0 + sc1) + sh1
        proj = h @ w_in[i]
        u = proj[..., :D_SSM]
        v = proj[..., D_SSM:D_SSM + D_CONV]
        gt = proj[..., D_SSM + D_CONV:]
        y_ssm = s5_mixer(u, lam_re[i], lam_im[i], log_dt[i], b_re[i], b_im[i], c_re[i],
                         c_im[i], d_skip[i], w_glu[i], b_glu[i])
        y_conv = conformer_conv(v, gt, conv_w[i], conv_b[i], ln_g[i], ln_b[i])
        merged = jnp.concatenate([rmsnorm(y_ssm, out_norm_g[i, :D_SSM]),
                                  rmsnorm(y_conv, out_norm_g[i, D_SSM:])], axis=-1)
        x = x + g1 * (merged @ w_out[i])
        h = rmsnorm(x, norm2_g[i]) * (1.0 + sc2) + sh2
        x = x + g2 * moe_ffn(h, router_w[i], router_b[i], w_gate[i], b_gate[i], w_up[i],
                             b_up[i], w_down[i], b_down[i])
    return rmsnorm(x, final_g)
```

```python
import functools
import math

import jax
import jax.numpy as jnp
from jax import lax
from jax.experimental import pallas as pl
from jax.experimental.pallas import tpu as pltpu

F32 = jnp.float32
BF16 = jnp.bfloat16
HIGHEST = lax.Precision.HIGHEST

D_MODEL = 1024
D_SSM = 512
D_CONV = 512
SSM_GROUP = 16
N_GROUPS = 32
SSM_STATE = 64
CONV_WIDTH = 31
N_EXPERTS = 32
TOP_K = 4
SWIGLU_ALPHA = 1.702
SWIGLU_LIMIT = 7.0
RMS_EPS = 1e-6
LN_EPS = 1e-5

CHUNK = 128
CONV_HALO = 32
VMEM_LIMIT = 56 * 1024 * 1024


def _rms(x, eps=RMS_EPS):
    return x * lax.rsqrt(jnp.mean(x * x, axis=-1, keepdims=True) + eps)


def _mod_kernel(c_ref, w_ref, b_ref, o_ref):
    c = c_ref[...]
    cond = c * jax.nn.sigmoid(c)
    o_ref[...] = jnp.dot(cond, w_ref[...], precision=HIGHEST, preferred_element_type=F32) + b_ref[...]


def _mod(c, ada_w, ada_b):
    bsz, d = c.shape
    n = ada_w.shape[1]
    return pl.pallas_call(
        _mod_kernel,
        grid=(n // d,),
        in_specs=[pl.BlockSpec((bsz, d), lambda j: (0, 0)),
                  pl.BlockSpec((d, d), lambda j: (0, j)),
                  pl.BlockSpec((1, d), lambda j: (0, j))],
        out_specs=pl.BlockSpec((bsz, d), lambda j: (0, j)),
        out_shape=jax.ShapeDtypeStruct((bsz, n), F32),
        name="mod",
    )(c, ada_w, ada_b.reshape(1, n))


def _inproj_kernel(x_ref, mod_ref, g1_ref, wu_ref, wvg_ref, cw_ref, cb_ref, lg_ref, lb_ref, og_ref,
                   ut_ref, nc_ref, zext, *, tm, rc):
    i = pl.program_id(1)
    x = x_ref[0]
    y = _rms(x) * g1_ref[...]
    h = (y * (1.0 + mod_ref[0, 1:2, :]) + mod_ref[0, 0:1, :]).astype(BF16)
    ut_ref[...] = lax.dot_general(wu_ref[...], h, (((1,), (1,)), ((), ())), preferred_element_type=F32)
    vg = jnp.dot(h, wvg_ref[...], preferred_element_type=F32)
    z = vg[:, :D_CONV] * jax.nn.sigmoid(vg[:, D_CONV:])

    @pl.when(i == 0)
    def _():
        zext[0:CONV_HALO, :] = jnp.zeros((CONV_HALO, D_CONV), F32)

    zext[CONV_HALO:CONV_HALO + tm, :] = z
    off = CONV_HALO - (CONV_WIDTH - 1)
    for r0 in range(0, tm, rc):
        acc = jnp.zeros((rc, D_CONV), F32) + cb_ref[...]
        for k in range(CONV_WIDTH):
            acc = acc + cw_ref[k:k + 1, :] * zext[r0 + off + k:r0 + off + k + rc, :]
        mu = jnp.mean(acc, axis=-1, keepdims=True)
        xc = acc - mu
        ln = xc * lax.rsqrt(jnp.mean(xc * xc, axis=-1, keepdims=True) + LN_EPS) * lg_ref[...] + lb_ref[...]
        act = ln * jax.nn.sigmoid(ln)
        nc_ref[0, r0:r0 + rc, :] = (_rms(act) * og_ref[...]).astype(BF16)
    zext[0:CONV_HALO, :] = zext[tm:tm + CONV_HALO, :]


def _inproj(x, mod3, norm1_g, wu_t, wvg, conv_w, conv_b, ln_g, ln_b, og_conv, *, tm, rc):
    bsz, seq, d = x.shape
    nt = seq // tm
    full = lambda shape: pl.BlockSpec(shape, lambda b, i: (0,) * len(shape))
    return pl.pallas_call(
        functools.partial(_inproj_kernel, tm=tm, rc=rc),
        grid=(bsz, nt),
        in_specs=[pl.BlockSpec((1, tm, d), lambda b, i: (b, i, 0)),
                  pl.BlockSpec((1, 6, d), lambda b, i: (b, 0, 0)),
                  full((1, d)), full((D_SSM, d)), full((d, 2 * D_CONV)),
                  full((CONV_HALO, D_CONV)), full((1, D_CONV)), full((1, D_CONV)), full((1, D_CONV)),
                  full((1, D_CONV))],
        out_specs=[pl.BlockSpec((D_SSM, tm), lambda b, i: (0, b * nt + i)),
                   pl.BlockSpec((1, tm, D_CONV), lambda b, i: (b, i, 0))],
        out_shape=[jax.ShapeDtypeStruct((D_SSM, bsz * seq), F32),
                   jax.ShapeDtypeStruct((bsz, seq, D_CONV), BF16)],
        scratch_shapes=[pltpu.VMEM((tm + CONV_HALO, D_CONV), F32)],
        compiler_params=pltpu.CompilerParams(dimension_semantics=("arbitrary", "arbitrary"),
                                             vmem_limit_bytes=VMEM_LIMIT),
        name="inproj",
    )(x, mod3, norm1_g, wu_t, wvg, conv_w, conv_b, ln_g, ln_b, og_conv)


def _ssm_kernel(u_ref, toep_ref, w_ref, v_ref, pq_ref, dsk_ref, y_ref, *, cpb):
    nc = u_ref.shape[2]
    us = [u_ref[0, h] for h in range(SSM_GROUP)]
    xcat = jnp.concatenate([u.astype(BF16) for u in us], axis=1)
    acc = jnp.dot(xcat, toep_ref[0], preferred_element_type=F32)
    st = jnp.dot(xcat, w_ref[0], preferred_element_type=F32)
    row = lax.broadcasted_iota(jnp.int32, (nc, 2 * SSM_STATE), 0) % cpb
    d = 1
    step = 0
    while d < cpb:
        sh = jnp.where(row >= d, pltpu.roll(st, d, axis=0), 0.0)
        st = st + pq_ref[0, step, 0:1, :] * sh + pq_ref[0, step, 1:2, :] * pltpu.roll(sh, SSM_STATE, axis=1)
        d *= 2
        step += 1
    prev = jnp.where(row >= 1, pltpu.roll(st, 1, axis=0), 0.0)
    acc = acc + jnp.dot(prev.astype(BF16), v_ref[0], preferred_element_type=F32)
    for h in range(SSM_GROUP):
        y_ref[0, h] = acc[:, h * CHUNK:(h + 1) * CHUNK] + dsk_ref[0, h] * us[h]


def _ssm(u4, toep, wst, vout, pq, dsk, *, cpb):
    g, hh, nc, t = u4.shape
    blk = lambda shape: pl.BlockSpec((1,) + shape, lambda i: (i,) + (0,) * len(shape))
    return pl.pallas_call(
        functools.partial(_ssm_kernel, cpb=cpb),
        grid=(g,),
        in_specs=[blk((hh, nc, t)), blk(toep.shape[1:]), blk(wst.shape[1:]), blk(vout.shape[1:]),
                  blk(pq.shape[1:]), blk(dsk.shape[1:])],
        out_specs=blk((hh, nc, t)),
        out_shape=jax.ShapeDtypeStruct(u4.shape, F32),
        compiler_params=pltpu.CompilerParams(dimension_semantics=("arbitrary",),
                                             vmem_limit_bytes=VMEM_LIMIT),
        name="ssm",
    )(u4, toep, wst, vout, pq, dsk)


def _ssm_params(lam_re, lam_im, log_dt, b_re, b_im, c_re, c_im, cpb):
    lr, li = lam_re.astype(F32), lam_im.astype(F32)
    dt = jnp.exp(log_dt.astype(F32))[:, None]
    mag = jnp.exp(lr * dt)
    ab_re = mag * jnp.cos(li * dt)
    ab_im = mag * jnp.sin(li * dt)
    den = lr * lr + li * li
    nr = ab_re - 1.0
    q_re = (nr * lr + ab_im * li) / den
    q_im = (ab_im * lr - nr * li) / den
    br, bi = b_re.astype(F32), b_im.astype(F32)
    bb_re = q_re[..., None] * br - q_im[..., None] * bi
    bb_im = q_re[..., None] * bi + q_im[..., None] * br
    cr, ci = c_re.astype(F32), c_im.astype(F32)

    pr, pi = jnp.ones((1,) + ab_re.shape, F32), jnp.zeros((1,) + ab_re.shape, F32)
    cur_r, cur_i = ab_re, ab_im
    while pr.shape[0] < CHUNK:
        pr, pi = (jnp.concatenate([pr, pr * cur_r - pi * cur_i], axis=0),
                  jnp.concatenate([pi, pr * cur_i + pi * cur_r], axis=0))
        cur_r, cur_i = cur_r * cur_r - cur_i * cur_i, 2.0 * cur_r * cur_i
    cp_re = cr[None] * pr[:, :, None, :] - ci[None] * pi[:, :, None, :]
    cp_im = cr[None] * pi[:, :, None, :] + ci[None] * pr[:, :, None, :]
    klag = (jnp.einsum('lghp,gpk->ghkl', cp_re, bb_re, precision=HIGHEST)
            - jnp.einsum('lghp,gpk->ghkl', cp_im, bb_im, precision=HIGHEST))
    s_idx = jnp.arange(CHUNK)[:, None]
    j_idx = jnp.arange(CHUNK)[None, :]
    lag = jnp.clip(j_idx - s_idx, 0, CHUNK - 1)
    toep = jnp.where(j_idx >= s_idx, klag[..., lag], 0.0)
    toep = toep.transpose(0, 2, 3, 1, 4).reshape(N_GROUPS, SSM_GROUP * CHUNK, SSM_GROUP * CHUNK).astype(BF16)
    rr, ri = pr[::-1], pi[::-1]
    w_re = rr[..., None] * bb_re[None] - ri[..., None] * bb_im[None]
    w_im = rr[..., None] * bb_im[None] + ri[..., None] * bb_re[None]
    wst = jnp.concatenate([w_re, w_im], axis=2).transpose(1, 3, 0, 2)
    wst = wst.reshape(N_GROUPS, SSM_GROUP * CHUNK, 2 * SSM_STATE).astype(BF16)
    p1r, p1i = pr * ab_re - pi * ab_im, pr * ab_im + pi * ab_re
    v_re = cr[None] * p1r[:, :, None, :] - ci[None] * p1i[:, :, None, :]
    v_im = cr[None] * p1i[:, :, None, :] + ci[None] * p1r[:, :, None, :]
    vout = jnp.concatenate([v_re, -v_im], axis=3).transpose(1, 3, 2, 0)
    vout = vout.reshape(N_GROUPS, 2 * SSM_STATE, SSM_GROUP * CHUNK).astype(BF16)
    tabs = []
    d = 1
    while d < cpb or not tabs:
        tabs.append(jnp.stack([jnp.concatenate([cur_r, cur_r], axis=1),
                               jnp.concatenate([-cur_i, cur_i], axis=1)], axis=1))
        cur_r, cur_i = cur_r * cur_r - cur_i * cur_i, 2.0 * cur_r * cur_i
        d *= 2
    pq = jnp.stack(tabs, axis=1)
    return toep, wst, vout, pq


def _mix_kernel(x_ref, yt_ref, nc_ref, mod_ref, wglu_ref, bglu_ref, ogs_ref, wot_ref, wob_ref, g2_ref,
                rw_ref, rb_ref, x1_ref, h2_ref, idx_ref, tw_ref):
    yt = yt_ref[...]
    yg = 0.5 * yt * (1.0 + jnp.tanh(math.sqrt(2.0 / math.pi) * (yt + 0.044715 * (yt * yt * yt))))
    gate = jnp.dot(wglu_ref[...], yg.astype(BF16), preferred_element_type=F32) + bglu_ref[...]
    y2 = yg * jax.nn.sigmoid(gate)
    ms = jnp.mean(y2 * y2, axis=0, keepdims=True)
    ns = (y2 * lax.rsqrt(ms + RMS_EPS) * ogs_ref[...]).astype(BF16)
    o = (lax.dot_general(ns, wot_ref[...], (((0,), (0,)), ((), ())), preferred_element_type=F32)
         + jnp.dot(nc_ref[0], wob_ref[...], preferred_element_type=F32))
    x1 = x_ref[0] + mod_ref[0, 2:3, :] * o
    x1_ref[0] = x1
    h2 = _rms(x1) * g2_ref[...] * (1.0 + mod_ref[0, 4:5, :]) + mod_ref[0, 3:4, :]
    h2_ref[...] = h2
    logits = jnp.dot(h2, rw_ref[...], precision=HIGHEST, preferred_element_type=F32) + rb_ref[...]
    lane = lax.broadcasted_iota(jnp.int32, logits.shape, 1)
    vals, idxs = [], []
    for _ in range(TOP_K):
        m = jnp.max(logits, axis=1, keepdims=True)
        sel = jnp.min(jnp.where(logits == m, lane, N_EXPERTS), axis=1, keepdims=True)
        vals.append(m)
        idxs.append(sel)
        logits = jnp.where(lane == sel, -jnp.inf, logits)
    es = [jnp.exp(v - vals[0]) for v in vals]
    tot = es[0] + es[1] + es[2] + es[3]
    idx_ref[...] = jnp.concatenate(idxs, axis=1)
    tw_ref[...] = jnp.concatenate([e / tot for e in es], axis=1)


def _mix(x, yt, nconv, mod3, wglu_t, bglu, og_ssm, wo_top, wo_bot, norm2_g, router_w, router_b, *, tm):
    bsz, seq, d = x.shape
    nt = seq // tm
    n_tok = bsz * seq
    full = lambda shape: pl.BlockSpec(shape, lambda b, i: (0,) * len(shape))
    return pl.pallas_call(
        _mix_kernel,
        grid=(bsz, nt),
        in_specs=[pl.BlockSpec((1, tm, d), lambda b, i: (b, i, 0)),
                  pl.BlockSpec((D_SSM, tm), lambda b, i: (0, b * nt + i)),
                  pl.BlockSpec((1, tm, D_CONV), lambda b, i: (b, i, 0)),
                  pl.BlockSpec((1, 6, d), lambda b, i: (b, 0, 0)),
                  full((D_SSM, D_SSM)), full((D_SSM, 1)), full((D_SSM, 1)),
                  full((D_SSM, d)), full((D_CONV, d)), full((1, d)),
                  full((d, N_EXPERTS)), full((1, N_EXPERTS))],
        out_specs=[pl.BlockSpec((1, tm, d), lambda b, i: (b, i, 0)),
                   pl.BlockSpec((tm, d), lambda b, i: (b * nt + i, 0)),
                   pl.BlockSpec((tm, TOP_K), lambda b, i: (b * nt + i, 0)),
                   pl.BlockSpec((tm, TOP_K), lambda b, i: (b * nt + i, 0))],
        out_shape=[jax.ShapeDtypeStruct((bsz, seq, d), F32),
                   jax.ShapeDtypeStruct((n_tok, d), F32),
                   jax.ShapeDtypeStruct((n_tok, TOP_K), jnp.int32),
                   jax.ShapeDtypeStruct((n_tok, TOP_K), F32)],
        compiler_params=pltpu.CompilerParams(dimension_semantics=("arbitrary", "arbitrary"),
                                             vmem_limit_bytes=VMEM_LIMIT),
        name="mix",
    )(x, yt, nconv, mod3, wglu_t, bglu, og_ssm, wo_top, wo_bot, norm2_g, router_w, router_b)


def _rank_kernel(idx_ref, rank_ref, cnt_ref, carry):
    i = pl.program_id(0)
    tp = idx_ref.shape[0]

    @pl.when(i == 0)
    def _():
        carry[...] = jnp.zeros_like(carry)

    idx = idx_ref[...]
    lane = lax.broadcasted_iota(jnp.int32, (tp, N_EXPERTS), 1)
    hot = jnp.zeros((tp, N_EXPERTS), F32)
    for k in range(TOP_K):
        hot = hot + jnp.where(lane == idx[:, k:k + 1], 1.0, 0.0)
    r = lax.broadcasted_iota(jnp.int32, (tp, tp), 0)
    c = lax.broadcasted_iota(jnp.int32, (tp, tp), 1)
    tri = jnp.where(c < r, 1.0, 0.0).astype(BF16)
    excl = jnp.dot(tri, hot.astype(BF16), preferred_element_type=F32) + carry[...]
    ranks = [jnp.sum(jnp.where(lane == idx[:, k:k + 1], excl, 0.0), axis=1, keepdims=True)
             for k in range(TOP_K)]
    rank_ref[...] = jnp.concatenate(ranks, axis=1).astype(jnp.int32)
    carry[...] = carry[...] + jnp.sum(hot, axis=0, keepdims=True)
    cnt_ref[...] = carry[...].astype(jnp.int32)


def _rank(top_idx, *, tp):
    n_tok = top_idx.shape[0]
    return pl.pallas_call(
        _rank_kernel,
        grid=(n_tok // tp,),
        in_specs=[pl.BlockSpec((tp, TOP_K), lambda i: (i, 0))],
        out_specs=[pl.BlockSpec((tp, TOP_K), lambda i: (i, 0)),
                   pl.BlockSpec((1, N_EXPERTS), lambda i: (0, 0))],
        out_shape=[jax.ShapeDtypeStruct((n_tok, TOP_K), jnp.int32),
                   jax.ShapeDtypeStruct((1, N_EXPERTS), jnp.int32)],
        scratch_shapes=[pltpu.VMEM((1, N_EXPERTS), F32)],
        compiler_params=pltpu.CompilerParams(dimension_semantics=("arbitrary",)),
        name="rank",
    )(top_idx)


def _dispatch_kernel(dest_ref, nv_ref, h_ref, xs_ref, zblk, sem, zsem, *, td, tb, n_blocks):
    i = pl.program_id(0)
    base = i * (td * TOP_K)

    @pl.when(i == 0)
    def _():
        zblk[...] = jnp.zeros_like(zblk)

        def zero_copy(b):
            return pltpu.make_async_copy(zblk, xs_ref.at[pl.ds(pl.multiple_of(b * tb, tb), tb), :], zsem)

        def start(b, carry):
            @pl.when(nv_ref[b] < tb)
            def _():
                zero_copy(b).start()
            return carry

        def wait(b, carry):
            @pl.when(nv_ref[b] < tb)
            def _():
                zero_copy(b).wait()
            return carry

        lax.fori_loop(0, n_blocks, start, 0)
        lax.fori_loop(0, n_blocks, wait, 0)

    def row_copy(t, k):
        return pltpu.make_async_copy(h_ref.at[pl.ds(t, 1), :],
                                     xs_ref.at[pl.ds(dest_ref[base + t * TOP_K + k], 1), :], sem)

    def issue(t, carry):
        for k in range(TOP_K):
            row_copy(t, k).start()
        return carry

    lax.fori_loop(0, td, issue, 0)

    def drain(t, carry):
        for k in range(TOP_K):
            row_copy(t, k).wait()
        return carry

    lax.fori_loop(0, td, drain, 0)


def _dispatch(dest_flat, block_nv, h2, *, td, tb):
    n_tok, d = h2.shape
    n_blocks = block_nv.shape[0]
    return pl.pallas_call(
        functools.partial(_dispatch_kernel, td=td, tb=tb, n_blocks=n_blocks),
        grid_spec=pltpu.PrefetchScalarGridSpec(
            num_scalar_prefetch=2,
            grid=(n_tok // td,),
            in_specs=[pl.BlockSpec((td, d), lambda i, dest, nv: (i, 0))],
            out_specs=pl.BlockSpec(memory_space=pl.ANY),
            scratch_shapes=[pltpu.VMEM((tb, d), F32), pltpu.SemaphoreType.DMA, pltpu.SemaphoreType.DMA]),
        out_shape=jax.ShapeDtypeStruct((n_blocks * tb, d), F32),
        compiler_params=pltpu.CompilerParams(dimension_semantics=("arbitrary",)),
        name="dispatch",
    )(dest_flat, block_nv, h2)


def _ffn_kernel(be_ref, nv_ref, xs_ref, wg_ref, bg_ref, wu_ref, bu_ref, wd_ref, bd_ref, ys_ref,
                wg_s, wu_s, wd_s):
    i = pl.program_id(0)
    nv = nv_ref[i]
    prev_e = be_ref[jnp.maximum(i - 1, 0)]

    @pl.when(nv > 0)
    def _():
        @pl.when((i == 0) | (be_ref[i] != prev_e))
        def _():
            wg_s[...] = wg_ref[0].astype(BF16)
            wu_s[...] = wu_ref[0].astype(BF16)
            wd_s[...] = wd_ref[0].astype(BF16)

        xb = xs_ref[...].astype(BF16)
        gate = jnp.minimum(jnp.dot(xb, wg_s[...], preferred_element_type=F32) + bg_ref[0], SWIGLU_LIMIT)
        up = jnp.clip(jnp.dot(xb, wu_s[...], preferred_element_type=F32) + bu_ref[0],
                      -SWIGLU_LIMIT, SWIGLU_LIMIT)
        glu = gate * jax.nn.sigmoid(SWIGLU_ALPHA * gate)
        act = ((up + 1.0) * glu).astype(BF16)
        ys_ref[...] = jnp.dot(act, wd_s[...], preferred_element_type=F32) + bd_ref[0]

    @pl.when(nv == 0)
    def _():
        ys_ref[...] = jnp.zeros_like(ys_ref)


def _ffn(block_e, block_nv, xs, w_gate, b_gate, w_up, b_up, w_down, b_down, *, tb):
    n_rows, d = xs.shape
    e, _, f = w_gate.shape
    wspec = lambda shape: pl.BlockSpec((1,) + shape, lambda i, be, nv: (be[i], 0, 0))
    return pl.pallas_call(
        _ffn_kernel,
        grid_spec=pltpu.PrefetchScalarGridSpec(
            num_scalar_prefetch=2,
            grid=(n_rows // tb,),
            in_specs=[pl.BlockSpec((tb, d), lambda i, be, nv: (i, 0)),
                      wspec((d, f)), wspec((1, f)), wspec((d, f)), wspec((1, f)),
                      wspec((f, d)), wspec((1, d))],
            out_specs=pl.BlockSpec((tb, d), lambda i, be, nv: (i, 0)),
            scratch_shapes=[pltpu.VMEM((d, f), BF16), pltpu.VMEM((d, f), BF16), pltpu.VMEM((f, d), BF16)]),
        out_shape=jax.ShapeDtypeStruct((n_rows, d), F32),
        compiler_params=pltpu.CompilerParams(dimension_semantics=("arbitrary",),
                                             vmem_limit_bytes=VMEM_LIMIT),
        name="ffn",
    )(block_e, block_nv, xs, w_gate, b_gate.reshape(e, 1, f), w_up, b_up.reshape(e, 1, f),
      w_down, b_down.reshape(e, 1, d))


def _combine_kernel(dest_ref, ys_ref, x1_ref, tw_ref, mod_ref, fg_ref, o_ref, buf, sem, *, tc):
    i = pl.program_id(0)
    base = i * (tc * TOP_K)

    def row_copy(t, k):
        return pltpu.make_async_copy(ys_ref.at[pl.ds(dest_ref[base + t * TOP_K + k], 1), :],
                                     buf.at[k, pl.ds(t, 1), :], sem)

    def issue(t, carry):
        for k in range(TOP_K):
            row_copy(t, k).start()
        return carry

    lax.fori_loop(0, tc, issue, 0)

    def drain(t, carry):
        for k in range(TOP_K):
            row_copy(t, k).wait()
        return carry

    lax.fori_loop(0, tc, drain, 0)

    tw = tw_ref[...]
    moe = tw[:, 0:1] * buf[0]
    for k in range(1, TOP_K):
        moe = moe + tw[:, k:k + 1] * buf[k]
    xo = x1_ref[...] + mod_ref[0, 5:6, :] * moe
    o_ref[...] = _rms(xo) * fg_ref[...]


def _combine(dest_flat, ys, x1, top_w, mod3, final_g, *, tc, seq):
    n_tok, d = x1.shape
    tiles_per_batch = seq // tc
    return pl.pallas_call(
        functools.partial(_combine_kernel, tc=tc),
        grid_spec=pltpu.PrefetchScalarGridSpec(
            num_scalar_prefetch=1,
            grid=(n_tok // tc,),
            in_specs=[pl.BlockSpec(memory_space=pl.ANY),
                      pl.BlockSpec((tc, d), lambda i, dest: (i, 0)),
                      pl.BlockSpec((tc, TOP_K), lambda i, dest: (i, 0)),
                      pl.BlockSpec((1, 6, d), lambda i, dest: (i // tiles_per_batch, 0, 0)),
                      pl.BlockSpec((1, d), lambda i, dest: (0, 0))],
            out_specs=pl.BlockSpec((tc, d), lambda i, dest: (i, 0)),
            scratch_shapes=[pltpu.VMEM((TOP_K, tc, d), F32), pltpu.SemaphoreType.DMA]),
        out_shape=jax.ShapeDtypeStruct((n_tok, d), F32),
        compiler_params=pltpu.CompilerParams(dimension_semantics=("arbitrary",)),
        name="combine",
    )(dest_flat, ys, x1, top_w, mod3, final_g)


def _forward(x, c, ada_w, ada_b, norm1_g, w_in, lam_re, lam_im, log_dt, b_re, b_im, c_re, c_im, d_skip,
             w_glu, b_glu, conv_w, conv_b, ln_g, ln_b, out_norm_g, w_out, norm2_g, router_w, router_b,
             w_gate, b_gate, w_up, b_up, w_down, b_down, final_g, *, tm, rc, tp, td, tb, tc):
    bsz, seq, d = x.shape
    n_tok = bsz * seq
    cpb = seq // CHUNK
    n_chunks = n_tok // CHUNK
    row = lambda v: v.reshape(1, -1).astype(F32)

    mod3 = _mod(c, ada_w, ada_b).reshape(bsz, 6, d)

    wu_t = w_in[:, :D_SSM].T.astype(BF16)
    wvg = w_in[:, D_SSM:].astype(BF16)
    cw = jnp.zeros((CONV_HALO, D_CONV), F32).at[:CONV_WIDTH].set(conv_w.astype(F32))
    ut, nconv = _inproj(x, mod3, row(norm1_g), wu_t, wvg, cw, row(conv_b), row(ln_g), row(ln_b),
                        row(out_norm_g[D_SSM:]), tm=tm, rc=rc)

    toep, wst, vout, pq = _ssm_params(lam_re, lam_im, log_dt, b_re, b_im, c_re, c_im, cpb)
    dsk = jnp.broadcast_to(d_skip.astype(F32).reshape(N_GROUPS, SSM_GROUP, 1, 1),
                           (N_GROUPS, SSM_GROUP, 1, CHUNK))
    y4 = _ssm(ut.reshape(N_GROUPS, SSM_GROUP, n_chunks, CHUNK), toep, wst, vout, pq, dsk, cpb=cpb)
    yt = y4.reshape(D_SSM, n_tok)

    x1, h2, top_idx, top_w = _mix(
        x, yt, nconv, mod3, w_glu.T.astype(BF16), b_glu.reshape(D_SSM, 1).astype(F32),
        out_norm_g[:D_SSM].reshape(D_SSM, 1).astype(F32), w_out[:D_SSM].astype(BF16),
        w_out[D_SSM:].astype(BF16), row(norm2_g), router_w.astype(F32), row(router_b), tm=tm)

    rank, counts = _rank(top_idx, tp=tp)
    counts = counts[0]
    padded = (counts + tb - 1) // tb * tb
    pend = jnp.cumsum(padded)
    pstart = pend - padded
    dest = (pstart[top_idx] + rank).reshape(-1)
    n_blocks = (n_tok * TOP_K) // tb + N_EXPERTS
    blk_start = jnp.arange(n_blocks, dtype=jnp.int32) * tb
    block_e = jnp.minimum(jnp.searchsorted(pend, blk_start, side='right'), N_EXPERTS - 1).astype(jnp.int32)
    block_nv = jnp.clip(pstart[block_e] + counts[block_e] - blk_start, 0, tb).astype(jnp.int32)
    block_nv = jnp.where(blk_start < pend[-1], block_nv, 0)

    xs = _dispatch(dest, block_nv, h2, td=td, tb=tb)
    ys = _ffn(block_e, block_nv, xs, w_gate, b_gate, w_up, b_up, w_down, b_down, tb=tb)
    out = _combine(dest, ys, x1.reshape(n_tok, d), top_w, mod3, row(final_g), tc=tc, seq=seq)
    return out.reshape(bsz, seq, d)


def kernel(x, c, ada_w, ada_b, norm1_g, w_in, lam_re, lam_im, log_dt, b_re, b_im, c_re, c_im, d_skip, w_glu, b_glu, conv_w, conv_b, ln_g, ln_b, out_norm_g, w_out, norm2_g, router_w, router_b, w_gate, b_gate, w_up, b_up, w_down, b_down, final_g):
    p = [a[0] for a in (ada_w, ada_b, norm1_g, w_in, lam_re, lam_im, log_dt, b_re, b_im, c_re, c_im, d_skip,
                        w_glu, b_glu, conv_w, conv_b, ln_g, ln_b, out_norm_g, w_out, norm2_g, router_w,
                        router_b, w_gate, b_gate, w_up, b_up, w_down, b_down)]
    return _forward(x, c, *p, final_g, tm=512, rc=64, tp=512, td=128, tb=512, tc=128)
```

```python
import functools
import math

import jax
import jax.numpy as jnp
from jax import lax
from jax.experimental import pallas as pl
from jax.experimental.pallas import tpu as pltpu

F32 = jnp.float32
BF16 = jnp.bfloat16
HIGHEST = lax.Precision.HIGHEST

D_MODEL = 1024
D_SSM = 512
D_CONV = 512
SSM_GROUP = 16
N_GROUPS = 32
SSM_STATE = 64
CONV_WIDTH = 31
N_EXPERTS = 32
TOP_K = 4
SWIGLU_ALPHA = 1.702
SWIGLU_LIMIT = 7.0
RMS_EPS = 1e-6
LN_EPS = 1e-5

CHUNK = 128
CONV_HALO = 32
VMEM_LIMIT = 56 * 1024 * 1024


def _rms(x, eps=RMS_EPS):
    return x * lax.rsqrt(jnp.mean(x * x, axis=-1, keepdims=True) + eps)


def _mod_kernel(c_ref, w_ref, b_ref, o_ref):
    c = c_ref[...]
    cond = c * jax.nn.sigmoid(c)
    o_ref[...] = jnp.dot(cond, w_ref[...], precision=HIGHEST, preferred_element_type=F32) + b_ref[...]


def _mod(c, ada_w, ada_b):
    bsz, d = c.shape
    n = ada_w.shape[1]
    return pl.pallas_call(
        _mod_kernel,
        grid=(n // d,),
        in_specs=[pl.BlockSpec((bsz, d), lambda j: (0, 0)),
                  pl.BlockSpec((d, d), lambda j: (0, j)),
                  pl.BlockSpec((1, d), lambda j: (0, j))],
        out_specs=pl.BlockSpec((bsz, d), lambda j: (0, j)),
        out_shape=jax.ShapeDtypeStruct((bsz, n), F32),
        name="mod",
    )(c, ada_w, ada_b.reshape(1, n))


def _inproj_kernel(x_ref, mod_ref, g1_ref, wu_ref, wvg_ref, cw_ref, cb_ref, lg_ref, lb_ref, og_ref,
                   ut_ref, nc_ref, zext, *, tm, rc):
    i = pl.program_id(1)
    x = x_ref[0]
    y = _rms(x) * g1_ref[...]
    h = (y * (1.0 + mod_ref[0, 1:2, :]) + mod_ref[0, 0:1, :]).astype(BF16)
    ut = lax.dot_general(wu_ref[...], h, (((1,), (1,)), ((), ())), preferred_element_type=F32)
    for cc in range(tm // CHUNK):
        ut_ref[:, cc, :] = ut[:, cc * CHUNK:(cc + 1) * CHUNK]
    vg = jnp.dot(h, wvg_ref[...], preferred_element_type=F32)
    z = vg[:, :D_CONV] * jax.nn.sigmoid(vg[:, D_CONV:])

    @pl.when(i == 0)
    def _():
        zext[0:CONV_HALO, :] = jnp.zeros((CONV_HALO, D_CONV), F32)

    zext[CONV_HALO:CONV_HALO + tm, :] = z
    off = CONV_HALO - (CONV_WIDTH - 1)
    for r0 in range(0, tm, rc):
        acc = jnp.zeros((rc, D_CONV), F32) + cb_ref[...]
        for k in range(CONV_WIDTH):
            acc = acc + cw_ref[k:k + 1, :] * zext[r0 + off + k:r0 + off + k + rc, :]
        mu = jnp.mean(acc, axis=-1, keepdims=True)
        xc = acc - mu
        ln = xc * lax.rsqrt(jnp.mean(xc * xc, axis=-1, keepdims=True) + LN_EPS) * lg_ref[...] + lb_ref[...]
        act = ln * jax.nn.sigmoid(ln)
        nc_ref[0, r0:r0 + rc, :] = (_rms(act) * og_ref[...]).astype(BF16)
    zext[0:CONV_HALO, :] = zext[tm:tm + CONV_HALO, :]


def _inproj(x, mod3, norm1_g, wu_t, wvg, conv_w, conv_b, ln_g, ln_b, og_conv, *, tm, rc):
    bsz, seq, d = x.shape
    nt = seq // tm
    full = lambda shape: pl.BlockSpec(shape, lambda b, i: (0,) * len(shape))
    return pl.pallas_call(
        functools.partial(_inproj_kernel, tm=tm, rc=rc),
        grid=(bsz, nt),
        in_specs=[pl.BlockSpec((1, tm, d), lambda b, i: (b, i, 0)),
                  pl.BlockSpec((1, 6, d), lambda b, i: (b, 0, 0)),
                  full((1, d)), full((D_SSM, d)), full((d, 2 * D_CONV)),
                  full((CONV_HALO, D_CONV)), full((1, D_CONV)), full((1, D_CONV)), full((1, D_CONV)),
                  full((1, D_CONV))],
        out_specs=[pl.BlockSpec((D_SSM, tm // CHUNK, CHUNK), lambda b, i: (0, b * nt + i, 0)),
                   pl.BlockSpec((1, tm, D_CONV), lambda b, i: (b, i, 0))],
        out_shape=[jax.ShapeDtypeStruct((D_SSM, bsz * seq // CHUNK, CHUNK), F32),
                   jax.ShapeDtypeStruct((bsz, seq, D_CONV), BF16)],
        scratch_shapes=[pltpu.VMEM((tm + CONV_HALO, D_CONV), F32)],
        compiler_params=pltpu.CompilerParams(dimension_semantics=("arbitrary", "arbitrary"),
                                             vmem_limit_bytes=VMEM_LIMIT),
        name="inproj",
    )(x, mod3, norm1_g, wu_t, wvg, conv_w, conv_b, ln_g, ln_b, og_conv)


def _ssm_kernel(u_ref, klag_ref, w_ref, v_ref, pq_ref, dsk_ref, y_ref, toep, *, cpb):
    nc = u_ref.shape[2]
    s_idx = lax.broadcasted_iota(jnp.int32, (CHUNK, CHUNK), 0)
    j_idx = lax.broadcasted_iota(jnp.int32, (CHUNK, CHUNK), 1)
    causal = j_idx >= s_idx

    def expand(hp, carry):
        kv = klag_ref[0, hp]
        blocks = []
        for h in range(SSM_GROUP):
            rows = jnp.broadcast_to(kv[h:h + 1, :], (CHUNK, CHUNK))
            skew = pltpu.roll(rows, 0, 1, stride=1, stride_axis=0)
            blocks.append(jnp.where(causal, skew, 0.0).astype(BF16))
        toep[pl.ds(pl.multiple_of(hp * CHUNK, CHUNK), CHUNK), :] = jnp.concatenate(blocks, axis=1)
        return carry

    lax.fori_loop(0, SSM_GROUP, expand, 0)

    us = [u_ref[0, h] for h in range(SSM_GROUP)]
    xcat = jnp.concatenate([u.astype(BF16) for u in us], axis=1)
    acc = jnp.dot(xcat, toep[...], preferred_element_type=F32)
    st = jnp.dot(xcat, w_ref[0], preferred_element_type=F32)
    row = lax.broadcasted_iota(jnp.int32, (nc, 2 * SSM_STATE), 0) % cpb
    d = 1
    step = 0
    while d < cpb:
        sh = jnp.where(row >= d, pltpu.roll(st, d, axis=0), 0.0)
        st = st + pq_ref[0, step, 0:1, :] * sh + pq_ref[0, step, 1:2, :] * pltpu.roll(sh, SSM_STATE, axis=1)
        d *= 2
        step += 1
    prev = jnp.where(row >= 1, pltpu.roll(st, 1, axis=0), 0.0)
    acc = acc + jnp.dot(prev.astype(BF16), v_ref[0], preferred_element_type=F32)
    for h in range(SSM_GROUP):
        y_ref[0, h] = acc[:, h * CHUNK:(h + 1) * CHUNK] + dsk_ref[0, h] * us[h]


def _ssm(u4, klag, wst, vout, pq, dsk, *, cpb):
    g, hh, nc, t = u4.shape
    blk = lambda shape: pl.BlockSpec((1,) + shape, lambda i: (i,) + (0,) * len(shape))
    return pl.pallas_call(
        functools.partial(_ssm_kernel, cpb=cpb),
        grid=(g,),
        in_specs=[blk((hh, nc, t)), blk(klag.shape[1:]), blk(wst.shape[1:]), blk(vout.shape[1:]),
                  blk(pq.shape[1:]), blk(dsk.shape[1:])],
        out_specs=blk((hh, nc, t)),
        out_shape=jax.ShapeDtypeStruct(u4.shape, F32),
        scratch_shapes=[pltpu.VMEM((hh * t, hh * t), BF16)],
        compiler_params=pltpu.CompilerParams(dimension_semantics=("arbitrary",),
                                             vmem_limit_bytes=VMEM_LIMIT),
        name="ssm",
    )(u4, klag, wst, vout, pq, dsk)


def _ssm_params(lam_re, lam_im, log_dt, b_re, b_im, c_re, c_im, cpb):
    lr, li = lam_re.astype(F32), lam_im.astype(F32)
    dt = jnp.exp(log_dt.astype(F32))[:, None]
    mag = jnp.exp(lr * dt)
    ab_re = mag * jnp.cos(li * dt)
    ab_im = mag * jnp.sin(li * dt)
    den = lr * lr + li * li
    nr = ab_re - 1.0
    q_re = (nr * lr + ab_im * li) / den
    q_im = (ab_im * lr - nr * li) / den
    br, bi = b_re.astype(F32), b_im.astype(F32)
    bb_re = q_re[..., None] * br - q_im[..., None] * bi
    bb_im = q_re[..., None] * bi + q_im[..., None] * br
    cr, ci = c_re.astype(F32), c_im.astype(F32)

    pr, pi = jnp.ones((1,) + ab_re.shape, F32), jnp.zeros((1,) + ab_re.shape, F32)
    cur_r, cur_i = ab_re, ab_im
    while pr.shape[0] < CHUNK:
        pr, pi = (jnp.concatenate([pr, pr * cur_r - pi * cur_i], axis=0),
                  jnp.concatenate([pi, pr * cur_i + pi * cur_r], axis=0))
        cur_r, cur_i = cur_r * cur_r - cur_i * cur_i, 2.0 * cur_r * cur_i
    cp_re = cr[None] * pr[:, :, None, :] - ci[None] * pi[:, :, None, :]
    cp_im = cr[None] * pi[:, :, None, :] + ci[None] * pr[:, :, None, :]
    klag = (jnp.einsum('lghp,gpk->gkhl', cp_re, bb_re, precision=HIGHEST)
            - jnp.einsum('lghp,gpk->gkhl', cp_im, bb_im, precision=HIGHEST))
    rr, ri = pr[::-1], pi[::-1]
    w_re = rr[..., None] * bb_re[None] - ri[..., None] * bb_im[None]
    w_im = rr[..., None] * bb_im[None] + ri[..., None] * bb_re[None]
    wst = jnp.concatenate([w_re, w_im], axis=2).transpose(1, 3, 0, 2)
    wst = wst.reshape(N_GROUPS, SSM_GROUP * CHUNK, 2 * SSM_STATE).astype(BF16)
    p1r, p1i = pr * ab_re - pi * ab_im, pr * ab_im + pi * ab_re
    v_re = cr[None] * p1r[:, :, None, :] - ci[None] * p1i[:, :, None, :]
    v_im = cr[None] * p1i[:, :, None, :] + ci[None] * p1r[:, :, None, :]
    vout = jnp.concatenate([v_re, -v_im], axis=3).transpose(1, 3, 2, 0)
    vout = vout.reshape(N_GROUPS, 2 * SSM_STATE, SSM_GROUP * CHUNK).astype(BF16)
    tabs = []
    d = 1
    while d < cpb or not tabs:
        tabs.append(jnp.stack([jnp.concatenate([cur_r, cur_r], axis=1),
                               jnp.concatenate([-cur_i, cur_i], axis=1)], axis=1))
        cur_r, cur_i = cur_r * cur_r - cur_i * cur_i, 2.0 * cur_r * cur_i
        d *= 2
    pq = jnp.stack(tabs, axis=1)
    return klag, wst, vout, pq


def _mix_kernel(x_ref, yt_ref, nc_ref, mod_ref, wglu_ref, bglu_ref, ogs_ref, wot_ref, wob_ref, g2_ref,
                rw_ref, rb_ref, x1_ref, h2_ref, idx_ref, tw_ref):
    yt = jnp.concatenate([yt_ref[:, cc, :] for cc in range(yt_ref.shape[1])], axis=1)
    yg = 0.5 * yt * (1.0 + jnp.tanh(math.sqrt(2.0 / math.pi) * (yt + 0.044715 * (yt * yt * yt))))
    gate = jnp.dot(wglu_ref[...], yg.astype(BF16), preferred_element_type=F32) + bglu_ref[...]
    y2 = yg * jax.nn.sigmoid(gate)
    ms = jnp.mean(y2 * y2, axis=0, keepdims=True)
    ns = (y2 * lax.rsqrt(ms + RMS_EPS) * ogs_ref[...]).astype(BF16)
    o = (lax.dot_general(ns, wot_ref[...], (((0,), (0,)), ((), ())), preferred_element_type=F32)
         + jnp.dot(nc_ref[0], wob_ref[...], preferred_element_type=F32))
    x1 = x_ref[0] + mod_ref[0, 2:3, :] * o
    x1_ref[0] = x1
    h2 = _rms(x1) * g2_ref[...] * (1.0 + mod_ref[0, 4:5, :]) + mod_ref[0, 3:4, :]
    h2_ref[...] = h2
    logits = jnp.dot(h2, rw_ref[...], precision=HIGHEST, preferred_element_type=F32) + rb_ref[...]
    lane = lax.broadcasted_iota(jnp.int32, logits.shape, 1)
    vals, idxs = [], []
    for _ in range(TOP_K):
        m = jnp.max(logits, axis=1, keepdims=True)
        sel = jnp.min(jnp.where(logits == m, lane, N_EXPERTS), axis=1, keepdims=True)
        vals.append(m)
        idxs.append(sel)
        logits = jnp.where(lane == sel, -jnp.inf, logits)
    es = [jnp.exp(v - vals[0]) for v in vals]
    tot = es[0] + es[1] + es[2] + es[3]
    idx_ref[...] = jnp.concatenate(idxs, axis=1)
    tw_ref[...] = jnp.concatenate([e / tot for e in es], axis=1)


def _mix(x, yt, nconv, mod3, wglu_t, bglu, og_ssm, wo_top, wo_bot, norm2_g, router_w, router_b, *, tm):
    bsz, seq, d = x.shape
    nt = seq // tm
    n_tok = bsz * seq
    full = lambda shape: pl.BlockSpec(shape, lambda b, i: (0,) * len(shape))
    return pl.pallas_call(
        _mix_kernel,
        grid=(bsz, nt),
        in_specs=[pl.BlockSpec((1, tm, d), lambda b, i: (b, i, 0)),
                  pl.BlockSpec((D_SSM, tm // CHUNK, CHUNK), lambda b, i: (0, b * nt + i, 0)),
                  pl.BlockSpec((1, tm, D_CONV), lambda b, i: (b, i, 0)),
                  pl.BlockSpec((1, 6, d), lambda b, i: (b, 0, 0)),
                  full((D_SSM, D_SSM)), full((D_SSM, 1)), full((D_SSM, 1)),
                  full((D_SSM, d)), full((D_CONV, d)), full((1, d)),
                  full((d, N_EXPERTS)), full((1, N_EXPERTS))],
        out_specs=[pl.BlockSpec((1, tm, d), lambda b, i: (b, i, 0)),
                   pl.BlockSpec((tm, d), lambda b, i: (b * nt + i, 0)),
                   pl.BlockSpec((tm, TOP_K), lambda b, i: (b * nt + i, 0)),
                   pl.BlockSpec((tm, TOP_K), lambda b, i: (b * nt + i, 0))],
        out_shape=[jax.ShapeDtypeStruct((bsz, seq, d), F32),
                   jax.ShapeDtypeStruct((n_tok, d), F32),
                   jax.ShapeDtypeStruct((n_tok, TOP_K), jnp.int32),
                   jax.ShapeDtypeStruct((n_tok, TOP_K), F32)],
        compiler_params=pltpu.CompilerParams(dimension_semantics=("arbitrary", "arbitrary"),
                                             vmem_limit_bytes=VMEM_LIMIT),
        name="mix",
    )(x, yt, nconv, mod3, wglu_t, bglu, og_ssm, wo_top, wo_bot, norm2_g, router_w, router_b)


def _rank_kernel(idx_ref, rank_ref, cnt_ref, carry):
    i = pl.program_id(0)
    tp = idx_ref.shape[0]

    @pl.when(i == 0)
    def _():
        carry[...] = jnp.zeros_like(carry)

    idx = idx_ref[...]
    lane = lax.broadcasted_iota(jnp.int32, (tp, N_EXPERTS), 1)
    hot = jnp.zeros((tp, N_EXPERTS), F32)
    for k in range(TOP_K):
        hot = hot + jnp.where(lane == idx[:, k:k + 1], 1.0, 0.0)
    r = lax.broadcasted_iota(jnp.int32, (tp, tp), 0)
    c = lax.broadcasted_iota(jnp.int32, (tp, tp), 1)
    tri = jnp.where(c < r, 1.0, 0.0).astype(BF16)
    excl = jnp.dot(tri, hot.astype(BF16), preferred_element_type=F32) + carry[...]
    ranks = [jnp.sum(jnp.where(lane == idx[:, k:k + 1], excl, 0.0), axis=1, keepdims=True)
             for k in range(TOP_K)]
    rank_ref[...] = jnp.concatenate(ranks, axis=1).astype(jnp.int32)
    carry[...] = carry[...] + jnp.sum(hot, axis=0, keepdims=True)
    cnt_ref[...] = carry[...].astype(jnp.int32)


def _rank(top_idx, *, tp):
    n_tok = top_idx.shape[0]
    return pl.pallas_call(
        _rank_kernel,
        grid=(n_tok // tp,),
        in_specs=[pl.BlockSpec((tp, TOP_K), lambda i: (i, 0))],
        out_specs=[pl.BlockSpec((tp, TOP_K), lambda i: (i, 0)),
                   pl.BlockSpec((1, N_EXPERTS), lambda i: (0, 0))],
        out_shape=[jax.ShapeDtypeStruct((n_tok, TOP_K), jnp.int32),
                   jax.ShapeDtypeStruct((1, N_EXPERTS), jnp.int32)],
        scratch_shapes=[pltpu.VMEM((1, N_EXPERTS), F32)],
        compiler_params=pltpu.CompilerParams(dimension_semantics=("arbitrary",)),
        name="rank",
    )(top_idx)


def _dispatch_kernel(dest_ref, nv_ref, h_ref, xs_ref, zblk, sem, zsem, *, td, tb, n_blocks):
    i = pl.program_id(0)
    base = i * (td * TOP_K)

    @pl.when(i == 0)
    def _():
        zblk[...] = jnp.zeros_like(zblk)

        def zero_copy(b):
            return pltpu.make_async_copy(zblk, xs_ref.at[pl.ds(pl.multiple_of(b * tb, tb), tb), :], zsem)

        def start(b, carry):
            @pl.when(nv_ref[b] < tb)
            def _():
                zero_copy(b).start()
            return carry

        def wait(b, carry):
            @pl.when(nv_ref[b] < tb)
            def _():
                zero_copy(b).wait()
            return carry

        lax.fori_loop(0, n_blocks, start, 0)
        lax.fori_loop(0, n_blocks, wait, 0)

    def row_copy(t, k):
        return pltpu.make_async_copy(h_ref.at[pl.ds(t, 1), :],
                                     xs_ref.at[pl.ds(dest_ref[base + t * TOP_K + k], 1), :], sem)

    def issue(t, carry):
        for k in range(TOP_K):
            row_copy(t, k).start()
        return carry

    lax.fori_loop(0, td, issue, 0)

    def drain(t, carry):
        for k in range(TOP_K):
            row_copy(t, k).wait()
        return carry

    lax.fori_loop(0, td, drain, 0)


def _dispatch(dest_flat, block_nv, h2, *, td, tb):
    n_tok, d = h2.shape
    n_blocks = block_nv.shape[0]
    return pl.pallas_call(
        functools.partial(_dispatch_kernel, td=td, tb=tb, n_blocks=n_blocks),
        grid_spec=pltpu.PrefetchScalarGridSpec(
            num_scalar_prefetch=2,
            grid=(n_tok // td,),
            in_specs=[pl.BlockSpec((td, d), lambda i, dest, nv: (i, 0))],
            out_specs=pl.BlockSpec(memory_space=pl.ANY),
            scratch_shapes=[pltpu.VMEM((tb, d), F32), pltpu.SemaphoreType.DMA, pltpu.SemaphoreType.DMA]),
        out_shape=jax.ShapeDtypeStruct((n_blocks * tb, d), F32),
        compiler_params=pltpu.CompilerParams(dimension_semantics=("arbitrary",)),
        name="dispatch",
    )(dest_flat, block_nv, h2)


def _ffn_kernel(be_ref, nv_ref, xs_ref, wg_ref, bg_ref, wu_ref, bu_ref, wd_ref, bd_ref, ys_ref,
                wg_s, wu_s, wd_s):
    i = pl.program_id(0)
    nv = nv_ref[i]
    prev_e = be_ref[jnp.maximum(i - 1, 0)]

    @pl.when(nv > 0)
    def _():
        @pl.when((i == 0) | (be_ref[i] != prev_e))
        def _():
            wg_s[...] = wg_ref[0].astype(BF16)
            wu_s[...] = wu_ref[0].astype(BF16)
            wd_s[...] = wd_ref[0].astype(BF16)

        xb = xs_ref[...].astype(BF16)
        gate = jnp.minimum(jnp.dot(xb, wg_s[...], preferred_element_type=F32) + bg_ref[0], SWIGLU_LIMIT)
        up = jnp.clip(jnp.dot(xb, wu_s[...], preferred_element_type=F32) + bu_ref[0],
                      -SWIGLU_LIMIT, SWIGLU_LIMIT)
        glu = gate * jax.nn.sigmoid(SWIGLU_ALPHA * gate)
        act = ((up + 1.0) * glu).astype(BF16)
        ys_ref[...] = jnp.dot(act, wd_s[...], preferred_element_type=F32) + bd_ref[0]

    @pl.when(nv == 0)
    def _():
        ys_ref[...] = jnp.zeros_like(ys_ref)


def _ffn(block_e, block_nv, xs, w_gate, b_gate, w_up, b_up, w_down, b_down, *, tb):
    n_rows, d = xs.shape
    e, _, f = w_gate.shape
    wspec = lambda shape: pl.BlockSpec((1,) + shape, lambda i, be, nv: (be[i], 0, 0))
    return pl.pallas_call(
        _ffn_kernel,
        grid_spec=pltpu.PrefetchScalarGridSpec(
            num_scalar_prefetch=2,
            grid=(n_rows // tb,),
            in_specs=[pl.BlockSpec((tb, d), lambda i, be, nv: (i, 0)),
                      wspec((d, f)), wspec((1, f)), wspec((d, f)), wspec((1, f)),
                      wspec((f, d)), wspec((1, d))],
            out_specs=pl.BlockSpec((tb, d), lambda i, be, nv: (i, 0)),
            scratch_shapes=[pltpu.VMEM((d, f), BF16), pltpu.VMEM((d, f), BF16), pltpu.VMEM((f, d), BF16)]),
        out_shape=jax.ShapeDtypeStruct((n_rows, d), F32),
        compiler_params=pltpu.CompilerParams(dimension_semantics=("arbitrary",),
                                             vmem_limit_bytes=VMEM_LIMIT),
        name="ffn",
    )(block_e, block_nv, xs, w_gate, b_gate.reshape(e, 1, f), w_up, b_up.reshape(e, 1, f),
      w_down, b_down.reshape(e, 1, d))


def _combine_kernel(dest_ref, ys_ref, x1_ref, tw_ref, mod_ref, fg_ref, o_ref, buf, sem, *, tc):
    i = pl.program_id(0)
    base = i * (tc * TOP_K)

    def row_copy(t, k):
        return pltpu.make_async_copy(ys_ref.at[pl.ds(dest_ref[base + t * TOP_K + k], 1), :],
                                     buf.at[k, pl.ds(t, 1), :], sem)

    def issue(t, carry):
        for k in range(TOP_K):
            row_copy(t, k).start()
        return carry

    lax.fori_loop(0, tc, issue, 0)

    def drain(t, carry):
        for k in range(TOP_K):
            row_copy(t, k).wait()
        return carry

    lax.fori_loop(0, tc, drain, 0)

    tw = tw_ref[...]
    moe = tw[:, 0:1] * buf[0]
    for k in range(1, TOP_K):
        moe = moe + tw[:, k:k + 1] * buf[k]
    xo = x1_ref[...] + mod_ref[0, 5:6, :] * moe
    o_ref[...] = _rms(xo) * fg_ref[...]


def _combine(dest_flat, ys, x1, top_w, mod3, final_g, *, tc, seq):
    n_tok, d = x1.shape
    tiles_per_batch = seq // tc
    return pl.pallas_call(
        functools.partial(_combine_kernel, tc=tc),
        grid_spec=pltpu.PrefetchScalarGridSpec(
            num_scalar_prefetch=1,
            grid=(n_tok // tc,),
            in_specs=[pl.BlockSpec(memory_space=pl.ANY),
                      pl.BlockSpec((tc, d), lambda i, dest: (i, 0)),
                      pl.BlockSpec((tc, TOP_K), lambda i, dest: (i, 0)),
                      pl.BlockSpec((1, 6, d), lambda i, dest: (i // tiles_per_batch, 0, 0)),
                      pl.BlockSpec((1, d), lambda i, dest: (0, 0))],
            out_specs=pl.BlockSpec((tc, d), lambda i, dest: (i, 0)),
            scratch_shapes=[pltpu.VMEM((TOP_K, tc, d), F32), pltpu.SemaphoreType.DMA]),
        out_shape=jax.ShapeDtypeStruct((n_tok, d), F32),
        compiler_params=pltpu.CompilerParams(dimension_semantics=("arbitrary",)),
        name="combine",
    )(dest_flat, ys, x1, top_w, mod3, final_g)


def _forward(x, c, ada_w, ada_b, norm1_g, w_in, lam_re, lam_im, log_dt, b_re, b_im, c_re, c_im, d_skip,
             w_glu, b_glu, conv_w, conv_b, ln_g, ln_b, out_norm_g, w_out, norm2_g, router_w, router_b,
             w_gate, b_gate, w_up, b_up, w_down, b_down, final_g, *, tm, rc, tp, td, tb, tc):
    bsz, seq, d = x.shape
    n_tok = bsz * seq
    cpb = seq // CHUNK
    n_chunks = n_tok // CHUNK
    row = lambda v: v.reshape(1, -1).astype(F32)

    mod3 = _mod(c, ada_w, ada_b).reshape(bsz, 6, d)

    wu_t = w_in[:, :D_SSM].T.astype(BF16)
    wvg = w_in[:, D_SSM:].astype(BF16)
    cw = jnp.zeros((CONV_HALO, D_CONV), F32).at[:CONV_WIDTH].set(conv_w.astype(F32))
    ut, nconv = _inproj(x, mod3, row(norm1_g), wu_t, wvg, cw, row(conv_b), row(ln_g), row(ln_b),
                        row(out_norm_g[D_SSM:]), tm=tm, rc=rc)

    klag, wst, vout, pq = _ssm_params(lam_re, lam_im, log_dt, b_re, b_im, c_re, c_im, cpb)
    dsk = jnp.broadcast_to(d_skip.astype(F32).reshape(N_GROUPS, SSM_GROUP, 1, 1),
                           (N_GROUPS, SSM_GROUP, 1, CHUNK))
    y4 = _ssm(ut.reshape(N_GROUPS, SSM_GROUP, n_chunks, CHUNK), klag, wst, vout, pq, dsk, cpb=cpb)
    yt = y4.reshape(D_SSM, n_chunks, CHUNK)

    x1, h2, top_idx, top_w = _mix(
        x, yt, nconv, mod3, w_glu.T.astype(BF16), b_glu.reshape(D_SSM, 1).astype(F32),
        out_norm_g[:D_SSM].reshape(D_SSM, 1).astype(F32), w_out[:D_SSM].astype(BF16),
        w_out[D_SSM:].astype(BF16), row(norm2_g), router_w.astype(F32), row(router_b), tm=tm)

    rank, counts = _rank(top_idx, tp=tp)
    counts = counts[0]
    padded = (counts + tb - 1) // tb * tb
    pend = jnp.cumsum(padded)
    pstart = pend - padded
    dest = (pstart[top_idx] + rank).reshape(-1)
    n_blocks = (n_tok * TOP_K) // tb + N_EXPERTS
    blk_start = jnp.arange(n_blocks, dtype=jnp.int32) * tb
    block_e = jnp.minimum(jnp.sum((blk_start[:, None] >= pend[None, :]).astype(jnp.int32), axis=1), N_EXPERTS - 1)
    block_nv = jnp.clip(pstart[block_e] + counts[block_e] - blk_start, 0, tb).astype(jnp.int32)
    block_nv = jnp.where(blk_start < pend[-1], block_nv, 0)

    xs = _dispatch(dest, block_nv, h2, td=td, tb=tb)
    ys = _ffn(block_e, block_nv, xs, w_gate, b_gate, w_up, b_up, w_down, b_down, tb=tb)
    out = _combine(dest, ys, x1.reshape(n_tok, d), top_w, mod3, row(final_g), tc=tc, seq=seq)
    return out.reshape(bsz, seq, d)


def kernel(x, c, ada_w, ada_b, norm1_g, w_in, lam_re, lam_im, log_dt, b_re, b_im, c_re, c_im, d_skip, w_glu, b_glu, conv_w, conv_b, ln_g, ln_b, out_norm_g, w_out, norm2_g, router_w, router_b, w_gate, b_gate, w_up, b_up, w_down, b_down, final_g):
    p = [a[0] for a in (ada_w, ada_b, norm1_g, w_in, lam_re, lam_im, log_dt, b_re, b_im, c_re, c_im, d_skip,
                        w_glu, b_glu, conv_w, conv_b, ln_g, ln_b, out_norm_g, w_out, norm2_g, router_w,
                        router_b, w_gate, b_gate, w_up, b_up, w_down, b_down)]
    return _forward(x, c, *p, final_g, tm=1024, rc=64, tp=512, td=128, tb=512, tc=128)
```

```python
import functools
import math

import jax
import jax.numpy as jnp
from jax import lax
from jax.experimental import pallas as pl
from jax.experimental.pallas import tpu as pltpu

F32 = jnp.float32
BF16 = jnp.bfloat16
HIGHEST = lax.Precision.HIGHEST

D_MODEL = 1024
D_SSM = 512
D_CONV = 512
SSM_GROUP = 16
N_GROUPS = 32
SSM_STATE = 64
CONV_WIDTH = 31
N_EXPERTS = 32
TOP_K = 4
SWIGLU_ALPHA = 1.702
SWIGLU_LIMIT = 7.0
RMS_EPS = 1e-6
LN_EPS = 1e-5

CHUNK = 128
CONV_HALO = 32
SUBLANES = 8
LANES = 128
VMEM_LIMIT = 56 * 1024 * 1024


def _rms(x, eps=RMS_EPS):
    return x * lax.rsqrt(jnp.mean(x * x, axis=-1, keepdims=True) + eps)


def _mod_kernel(c_ref, w_ref, b_ref, o_ref):
    c = c_ref[...]
    cond = c * jax.nn.sigmoid(c)
    o_ref[...] = jnp.dot(cond, w_ref[...], precision=HIGHEST, preferred_element_type=F32) + b_ref[...]


def _mod(c, ada_w, ada_b):
    bsz, d = c.shape
    n = ada_w.shape[1]
    return pl.pallas_call(
        _mod_kernel,
        grid=(n // d,),
        in_specs=[pl.BlockSpec((bsz, d), lambda j: (0, 0)),
                  pl.BlockSpec((d, d), lambda j: (0, j)),
                  pl.BlockSpec((1, d), lambda j: (0, j))],
        out_specs=pl.BlockSpec((bsz, d), lambda j: (0, j)),
        out_shape=jax.ShapeDtypeStruct((bsz, n), F32),
        name="mod",
    )(c, ada_w, ada_b.reshape(1, n))


def _inproj_kernel(x_ref, mod_ref, g1_ref, wu_ref, wvg_ref, cw_ref, cb_ref, lg_ref, lb_ref, og_ref,
                   ut_ref, nc_ref, zext, zsh, *, tm, rc):
    i = pl.program_id(1)
    x = x_ref[0]
    y = _rms(x) * g1_ref[...]
    h = (y * (1.0 + mod_ref[0, 1:2, :]) + mod_ref[0, 0:1, :]).astype(BF16)
    ut = lax.dot_general(wu_ref[...], h, (((1,), (1,)), ((), ())), preferred_element_type=F32)
    for cc in range(tm // CHUNK):
        ut_ref[:, cc, :] = ut[:, cc * CHUNK:(cc + 1) * CHUNK]
    vg = jnp.dot(h, wvg_ref[...], preferred_element_type=F32)
    z = vg[:, :D_CONV] * jax.nn.sigmoid(vg[:, D_CONV:])

    @pl.when(i == 0)
    def _():
        zext[0:CONV_HALO, :] = jnp.zeros((CONV_HALO, D_CONV), F32)

    zext[CONV_HALO:CONV_HALO + tm, :] = z
    off = CONV_HALO - (CONV_WIDTH - 1)
    span = tm + CONV_HALO - SUBLANES
    for p in range(1, SUBLANES):
        zsh[p - 1, 0:span, :] = zext[p:p + span, :]
    for r0 in range(0, tm, rc):
        acc = jnp.zeros((rc, D_CONV), F32) + cb_ref[...]
        for k in range(CONV_WIDTH):
            p = (off + k) % SUBLANES
            lo = r0 + off + k - p
            tap = zext[lo:lo + rc, :] if p == 0 else zsh[p - 1, lo:lo + rc, :]
            acc = acc + cw_ref[k:k + 1, :] * tap
        mu = jnp.mean(acc, axis=-1, keepdims=True)
        xc = acc - mu
        ln = xc * lax.rsqrt(jnp.mean(xc * xc, axis=-1, keepdims=True) + LN_EPS) * lg_ref[...] + lb_ref[...]
        act = ln * jax.nn.sigmoid(ln)
        nc_ref[0, r0:r0 + rc, :] = (_rms(act) * og_ref[...]).astype(BF16)
    zext[0:CONV_HALO, :] = zext[tm:tm + CONV_HALO, :]


def _inproj(x, mod3, norm1_g, wu_t, wvg, conv_w, conv_b, ln_g, ln_b, og_conv, *, tm, rc):
    bsz, seq, d = x.shape
    nt = seq // tm
    full = lambda shape: pl.BlockSpec(shape, lambda b, i: (0,) * len(shape))
    return pl.pallas_call(
        functools.partial(_inproj_kernel, tm=tm, rc=rc),
        grid=(bsz, nt),
        in_specs=[pl.BlockSpec((1, tm, d), lambda b, i: (b, i, 0)),
                  pl.BlockSpec((1, 6, d), lambda b, i: (b, 0, 0)),
                  full((1, d)), full((D_SSM, d)), full((d, 2 * D_CONV)),
                  full((CONV_HALO, D_CONV)), full((1, D_CONV)), full((1, D_CONV)), full((1, D_CONV)),
                  full((1, D_CONV))],
        out_specs=[pl.BlockSpec((D_SSM, tm // CHUNK, CHUNK), lambda b, i: (0, b * nt + i, 0)),
                   pl.BlockSpec((1, tm, D_CONV), lambda b, i: (b, i, 0))],
        out_shape=[jax.ShapeDtypeStruct((D_SSM, bsz * seq // CHUNK, CHUNK), F32),
                   jax.ShapeDtypeStruct((bsz, seq, D_CONV), BF16)],
        scratch_shapes=[pltpu.VMEM((tm + CONV_HALO, D_CONV), F32),
                        pltpu.VMEM((SUBLANES - 1, tm + CONV_HALO, D_CONV), F32)],
        compiler_params=pltpu.CompilerParams(dimension_semantics=("arbitrary", "arbitrary"),
                                             vmem_limit_bytes=VMEM_LIMIT),
        name="inproj",
    )(x, mod3, norm1_g, wu_t, wvg, conv_w, conv_b, ln_g, ln_b, og_conv)


def _ssm_kernel(u_ref, klag_ref, w_ref, v_ref, pq_ref, dsk_ref, y_ref, toep, *, cpb):
    nc = u_ref.shape[2]
    s_idx = lax.broadcasted_iota(jnp.int32, (CHUNK, CHUNK), 0)
    j_idx = lax.broadcasted_iota(jnp.int32, (CHUNK, CHUNK), 1)
    causal = j_idx >= s_idx

    def expand(hp, carry):
        kv = klag_ref[0, hp]
        blocks = []
        for h in range(SSM_GROUP):
            rows = jnp.broadcast_to(kv[h:h + 1, :], (CHUNK, CHUNK))
            skew = pltpu.roll(rows, 0, 1, stride=1, stride_axis=0)
            blocks.append(jnp.where(causal, skew, 0.0).astype(BF16))
        toep[pl.ds(pl.multiple_of(hp * CHUNK, CHUNK), CHUNK), :] = jnp.concatenate(blocks, axis=1)
        return carry

    lax.fori_loop(0, SSM_GROUP, expand, 0)

    us = [u_ref[0, h] for h in range(SSM_GROUP)]
    xcat = jnp.concatenate([u.astype(BF16) for u in us], axis=1)
    acc = jnp.dot(xcat, toep[...], preferred_element_type=F32)
    st = jnp.dot(xcat, w_ref[0], preferred_element_type=F32)
    row = lax.broadcasted_iota(jnp.int32, (nc, 2 * SSM_STATE), 0) % cpb
    d = 1
    step = 0
    while d < cpb:
        sh = jnp.where(row >= d, pltpu.roll(st, d, axis=0), 0.0)
        st = st + pq_ref[0, step, 0:1, :] * sh + pq_ref[0, step, 1:2, :] * pltpu.roll(sh, SSM_STATE, axis=1)
        d *= 2
        step += 1
    prev = jnp.where(row >= 1, pltpu.roll(st, 1, axis=0), 0.0)
    acc = acc + jnp.dot(prev.astype(BF16), v_ref[0], preferred_element_type=F32)
    for h in range(SSM_GROUP):
        y_ref[0, h] = acc[:, h * CHUNK:(h + 1) * CHUNK] + dsk_ref[0, h] * us[h]


def _ssm(u4, klag, wst, vout, pq, dsk, *, cpb):
    g, hh, nc, t = u4.shape
    blk = lambda shape: pl.BlockSpec((1,) + shape, lambda i: (i,) + (0,) * len(shape))
    return pl.pallas_call(
        functools.partial(_ssm_kernel, cpb=cpb),
        grid=(g,),
        in_specs=[blk((hh, nc, t)), blk(klag.shape[1:]), blk(wst.shape[1:]), blk(vout.shape[1:]),
                  blk(pq.shape[1:]), blk(dsk.shape[1:])],
        out_specs=blk((hh, nc, t)),
        out_shape=jax.ShapeDtypeStruct(u4.shape, F32),
        scratch_shapes=[pltpu.VMEM((hh * t, hh * t), BF16)],
        compiler_params=pltpu.CompilerParams(dimension_semantics=("arbitrary",),
                                             vmem_limit_bytes=VMEM_LIMIT),
        name="ssm",
    )(u4, klag, wst, vout, pq, dsk)


def _ssm_params(lam_re, lam_im, log_dt, b_re, b_im, c_re, c_im, cpb):
    lr, li = lam_re.astype(F32), lam_im.astype(F32)
    dt = jnp.exp(log_dt.astype(F32))[:, None]
    mag = jnp.exp(lr * dt)
    ab_re = mag * jnp.cos(li * dt)
    ab_im = mag * jnp.sin(li * dt)
    den = lr * lr + li * li
    nr = ab_re - 1.0
    q_re = (nr * lr + ab_im * li) / den
    q_im = (ab_im * lr - nr * li) / den
    br, bi = b_re.astype(F32), b_im.astype(F32)
    bb_re = q_re[..., None] * br - q_im[..., None] * bi
    bb_im = q_re[..., None] * bi + q_im[..., None] * br
    cr, ci = c_re.astype(F32), c_im.astype(F32)

    pr, pi = jnp.ones((1,) + ab_re.shape, F32), jnp.zeros((1,) + ab_re.shape, F32)
    cur_r, cur_i = ab_re, ab_im
    while pr.shape[0] < CHUNK:
        pr, pi = (jnp.concatenate([pr, pr * cur_r - pi * cur_i], axis=0),
                  jnp.concatenate([pi, pr * cur_i + pi * cur_r], axis=0))
        cur_r, cur_i = cur_r * cur_r - cur_i * cur_i, 2.0 * cur_r * cur_i
    cp_re = cr[None] * pr[:, :, None, :] - ci[None] * pi[:, :, None, :]
    cp_im = cr[None] * pi[:, :, None, :] + ci[None] * pr[:, :, None, :]
    klag = (jnp.einsum('lghp,gpk->gkhl', cp_re, bb_re, precision=HIGHEST)
            - jnp.einsum('lghp,gpk->gkhl', cp_im, bb_im, precision=HIGHEST))
    rr, ri = pr[::-1], pi[::-1]
    w_re = rr[..., None] * bb_re[None] - ri[..., None] * bb_im[None]
    w_im = rr[..., None] * bb_im[None] + ri[..., None] * bb_re[None]
    wst = jnp.concatenate([w_re, w_im], axis=2).transpose(1, 3, 0, 2)
    wst = wst.reshape(N_GROUPS, SSM_GROUP * CHUNK, 2 * SSM_STATE).astype(BF16)
    p1r, p1i = pr * ab_re - pi * ab_im, pr * ab_im + pi * ab_re
    v_re = cr[None] * p1r[:, :, None, :] - ci[None] * p1i[:, :, None, :]
    v_im = cr[None] * p1i[:, :, None, :] + ci[None] * p1r[:, :, None, :]
    vout = jnp.concatenate([v_re, -v_im], axis=3).transpose(1, 3, 2, 0)
    vout = vout.reshape(N_GROUPS, 2 * SSM_STATE, SSM_GROUP * CHUNK).astype(BF16)
    tabs = []
    d = 1
    while d < cpb or not tabs:
        tabs.append(jnp.stack([jnp.concatenate([cur_r, cur_r], axis=1),
                               jnp.concatenate([-cur_i, cur_i], axis=1)], axis=1))
        cur_r, cur_i = cur_r * cur_r - cur_i * cur_i, 2.0 * cur_r * cur_i
        d *= 2
    pq = jnp.stack(tabs, axis=1)
    return klag, wst, vout, pq


def _mix_kernel(x_ref, yt_ref, nc_ref, mod_ref, wglu_ref, bglu_ref, ogs_ref, wot_ref, wob_ref, g2_ref,
                rw_ref, rb_ref, x1_ref, h2_ref, idx_ref, tw_ref):
    yt = jnp.concatenate([yt_ref[:, cc, :] for cc in range(yt_ref.shape[1])], axis=1)
    yg = 0.5 * yt * (1.0 + jnp.tanh(math.sqrt(2.0 / math.pi) * (yt + 0.044715 * (yt * yt * yt))))
    gate = jnp.dot(wglu_ref[...], yg.astype(BF16), preferred_element_type=F32) + bglu_ref[...]
    y2 = yg * jax.nn.sigmoid(gate)
    ms = jnp.mean(y2 * y2, axis=0, keepdims=True)
    ns = (y2 * lax.rsqrt(ms + RMS_EPS) * ogs_ref[...]).astype(BF16)
    o = (lax.dot_general(ns, wot_ref[...], (((0,), (0,)), ((), ())), preferred_element_type=F32)
         + jnp.dot(nc_ref[0], wob_ref[...], preferred_element_type=F32))
    x1 = x_ref[0] + mod_ref[0, 2:3, :] * o
    x1_ref[0] = x1
    h2 = _rms(x1) * g2_ref[...] * (1.0 + mod_ref[0, 4:5, :]) + mod_ref[0, 3:4, :]
    rpt = h2.shape[1] // LANES
    for j in range(rpt):
        h2_ref[pl.ds(j, h2.shape[0], stride=rpt), :] = h2[:, j * LANES:(j + 1) * LANES]
    h2_hi = h2.astype(BF16)
    h2_lo = (h2 - h2_hi.astype(F32)).astype(BF16)
    l_hi = jnp.dot(h2_hi, rw_ref[...], preferred_element_type=F32)
    l_lo = jnp.dot(h2_lo, rw_ref[:, :N_EXPERTS], preferred_element_type=F32)
    logits = l_hi[:, :N_EXPERTS] + (l_hi[:, N_EXPERTS:] + l_lo) + rb_ref[...]
    lane = lax.broadcasted_iota(jnp.int32, logits.shape, 1)
    vals, idxs = [], []
    for _ in range(TOP_K):
        m = jnp.max(logits, axis=1, keepdims=True)
        sel = jnp.min(jnp.where(logits == m, lane, N_EXPERTS), axis=1, keepdims=True)
        vals.append(m)
        idxs.append(sel)
        logits = jnp.where(lane == sel, -jnp.inf, logits)
    es = [jnp.exp(v - vals[0]) for v in vals]
    tot = es[0] + es[1] + es[2] + es[3]
    idx_ref[...] = jnp.concatenate(idxs, axis=1)
    tw_ref[...] = jnp.concatenate([e / tot for e in es], axis=1)


def _mix(x, yt, nconv, mod3, wglu_t, bglu, og_ssm, wo_top, wo_bot, norm2_g, router_w, router_b, *, tm):
    bsz, seq, d = x.shape
    nt = seq // tm
    n_tok = bsz * seq
    full = lambda shape: pl.BlockSpec(shape, lambda b, i: (0,) * len(shape))
    return pl.pallas_call(
        _mix_kernel,
        grid=(bsz, nt),
        in_specs=[pl.BlockSpec((1, tm, d), lambda b, i: (b, i, 0)),
                  pl.BlockSpec((D_SSM, tm // CHUNK, CHUNK), lambda b, i: (0, b * nt + i, 0)),
                  pl.BlockSpec((1, tm, D_CONV), lambda b, i: (b, i, 0)),
                  pl.BlockSpec((1, 6, d), lambda b, i: (b, 0, 0)),
                  full((D_SSM, D_SSM)), full((D_SSM, 1)), full((D_SSM, 1)),
                  full((D_SSM, d)), full((D_CONV, d)), full((1, d)),
                  full((d, 2 * N_EXPERTS)), full((1, N_EXPERTS))],
        out_specs=[pl.BlockSpec((1, tm, d), lambda b, i: (b, i, 0)),
                   pl.BlockSpec((tm * d // LANES, LANES), lambda b, i: (b * nt + i, 0)),
                   pl.BlockSpec((tm, TOP_K), lambda b, i: (b * nt + i, 0)),
                   pl.BlockSpec((tm, TOP_K), lambda b, i: (b * nt + i, 0))],
        out_shape=[jax.ShapeDtypeStruct((bsz, seq, d), F32),
                   jax.ShapeDtypeStruct((n_tok * d // LANES, LANES), F32),
                   jax.ShapeDtypeStruct((n_tok, TOP_K), jnp.int32),
                   jax.ShapeDtypeStruct((n_tok, TOP_K), F32)],
        compiler_params=pltpu.CompilerParams(dimension_semantics=("arbitrary", "arbitrary"),
                                             vmem_limit_bytes=VMEM_LIMIT),
        name="mix",
    )(x, yt, nconv, mod3, wglu_t, bglu, og_ssm, wo_top, wo_bot, norm2_g, router_w, router_b)


def _rank_kernel(idx_ref, rank_ref, cnt_ref, carry):
    i = pl.program_id(0)
    tp = idx_ref.shape[0]

    @pl.when(i == 0)
    def _():
        carry[...] = jnp.zeros_like(carry)

    idx = idx_ref[...]
    lane = lax.broadcasted_iota(jnp.int32, (tp, N_EXPERTS), 1)
    hot = jnp.zeros((tp, N_EXPERTS), F32)
    for k in range(TOP_K):
        hot = hot + jnp.where(lane == idx[:, k:k + 1], 1.0, 0.0)
    r = lax.broadcasted_iota(jnp.int32, (tp, tp), 0)
    c = lax.broadcasted_iota(jnp.int32, (tp, tp), 1)
    tri = jnp.where(c < r, 1.0, 0.0).astype(BF16)
    excl = jnp.dot(tri, hot.astype(BF16), preferred_element_type=F32) + carry[...]
    ranks = [jnp.sum(jnp.where(lane == idx[:, k:k + 1], excl, 0.0), axis=1, keepdims=True)
             for k in range(TOP_K)]
    rank_ref[...] = jnp.concatenate(ranks, axis=1).astype(jnp.int32)
    carry[...] = carry[...] + jnp.sum(hot, axis=0, keepdims=True)
    cnt_ref[...] = carry[...].astype(jnp.int32)


def _rank(top_idx, *, tp):
    n_tok = top_idx.shape[0]
    return pl.pallas_call(
        _rank_kernel,
        grid=(n_tok // tp,),
        in_specs=[pl.BlockSpec((tp, TOP_K), lambda i: (i, 0))],
        out_specs=[pl.BlockSpec((tp, TOP_K), lambda i: (i, 0)),
                   pl.BlockSpec((1, N_EXPERTS), lambda i: (0, 0))],
        out_shape=[jax.ShapeDtypeStruct((n_tok, TOP_K), jnp.int32),
                   jax.ShapeDtypeStruct((1, N_EXPERTS), jnp.int32)],
        scratch_shapes=[pltpu.VMEM((1, N_EXPERTS), F32)],
        compiler_params=pltpu.CompilerParams(dimension_semantics=("arbitrary",)),
        name="rank",
    )(top_idx)


def _moe_kernel(be_ref, nv_ref, inv_cur_ref, inv_nxt_ref, h_ref, wg_ref, bg_ref, wu_ref, bu_ref, wd_ref, bd_ref,
                ys_ref, xbuf, ybuf, slots0, slots1, wg_s, wu_s, wd_s, gsem, ssem, isem,
                *, tb, fc, n_blocks, n_tok):
    i = pl.program_id(0)
    b = i % 2
    nv = nv_ref[i]
    f = wg_s.shape[1]
    d = f
    rpt = d // LANES
    slot_bufs = (slots0, slots1)

    def gather_row(buf, r):
        tok = slot_bufs[buf][r] & (n_tok - 1)
        return pltpu.make_async_copy(h_ref.at[pl.ds(pl.multiple_of(tok * rpt, rpt), rpt), :],
                                     xbuf.at[buf, pl.ds(pl.multiple_of(r * rpt, rpt), rpt), :], gsem.at[buf])

    def scatter_row(buf, r):
        return pltpu.make_async_copy(ybuf.at[buf, pl.ds(pl.multiple_of(r * rpt, rpt), rpt), :],
                                     ys_ref.at[pl.ds(pl.multiple_of(slot_bufs[buf][r] * rpt, rpt), rpt), :],
                                     ssem.at[buf])

    def gather_all(buf):
        return pltpu.make_async_copy(h_ref.at[pl.ds(0, tb * rpt), :], xbuf.at[buf], gsem.at[buf])

    def scatter_all(buf):
        return pltpu.make_async_copy(ybuf.at[buf], ys_ref.at[pl.ds(0, tb * rpt), :], ssem.at[buf])

    def on_buffer(which, fn):
        for buf in (0, 1):
            @pl.when(which == buf)
            def _():
                fn(buf)

    def start_rows(n, row_copy):
        @pl.when(n == tb)
        def _():
            def body(g, carry):
                for u in range(SUBLANES):
                    row_copy(g * SUBLANES + u).start()
                return carry
            lax.fori_loop(0, tb // SUBLANES, body, 0)

        @pl.when(n < tb)
        def _():
            def body(r, carry):
                row_copy(r).start()
                return carry
            lax.fori_loop(0, n, body, 0)

    def wait_rows(n, all_copy, row_copy):
        @pl.when(n == tb)
        def _():
            all_copy.wait()

        @pl.when(n < tb)
        def _():
            def body(r, carry):
                row_copy.wait()
                return carry
            lax.fori_loop(0, n, body, 0)

    def load_slots(src_ref, buf):
        cp = pltpu.make_async_copy(src_ref.at[0, 0], slot_bufs[buf], isem)
        cp.start()
        cp.wait()

    @pl.when(i == 0)
    def _():
        xbuf[...] = jnp.zeros_like(xbuf)
        load_slots(inv_cur_ref, 0)
        start_rows(nv, lambda r: gather_row(0, r))

    nv_next = jnp.where(i + 1 < n_blocks, nv_ref[jnp.minimum(i + 1, n_blocks - 1)], 0)

    def prefetch_next(buf):
        load_slots(inv_nxt_ref, buf)
        start_rows(nv_next, lambda r: gather_row(buf, r))

    @pl.when(nv_next > 0)
    def _():
        on_buffer(1 - b, prefetch_next)

    nv_prev2 = jnp.where(i >= 2, nv_ref[jnp.maximum(i - 2, 0)], 0)

    def wait_inputs(buf):
        wait_rows(nv, gather_all(buf), gather_row(buf, 0))
        wait_rows(nv_prev2, scatter_all(buf), scatter_row(buf, 0))

    on_buffer(b, wait_inputs)

    @pl.when(nv > 0)
    def _():
        @pl.when((i == 0) | (be_ref[i] != be_ref[jnp.maximum(i - 1, 0)]))
        def _():
            wg_s[...] = wg_ref[0].astype(BF16)
            wu_s[...] = wu_ref[0].astype(BF16)
            wd_s[...] = wd_ref[0].astype(BF16)

        xb = jnp.concatenate([xbuf[b, pl.ds(j, tb, stride=rpt), :] for j in range(rpt)], axis=1).astype(BF16)
        y = None
        for j in range(f // fc):
            cols = slice(j * fc, (j + 1) * fc)
            gate = jnp.minimum(jnp.dot(xb, wg_s[:, cols], preferred_element_type=F32) + bg_ref[0, :, cols],
                               SWIGLU_LIMIT)
            up = jnp.clip(jnp.dot(xb, wu_s[:, cols], preferred_element_type=F32) + bu_ref[0, :, cols],
                          -SWIGLU_LIMIT, SWIGLU_LIMIT)
            act = ((up + 1.0) * (gate * jax.nn.sigmoid(SWIGLU_ALPHA * gate))).astype(BF16)
            part = jnp.dot(act, wd_s[cols, :], preferred_element_type=F32)
            y = part if y is None else y + part
        y = y + bd_ref[0]
        for j in range(rpt):
            ybuf[b, pl.ds(j, tb, stride=rpt), :] = y[:, j * LANES:(j + 1) * LANES]

    on_buffer(b, lambda buf: start_rows(nv, lambda r: scatter_row(buf, r)))

    @pl.when(i == n_blocks - 1)
    def _():
        nv_prev1 = jnp.where(i >= 1, nv_ref[jnp.maximum(i - 1, 0)], 0)
        on_buffer(1 - b, lambda buf: wait_rows(nv_prev1, scatter_all(buf), scatter_row(buf, 0)))
        on_buffer(b, lambda buf: wait_rows(nv, scatter_all(buf), scatter_row(buf, 0)))


def _moe(block_e, block_nv, inv, h2t, w_gate, b_gate, w_up, b_up, w_down, b_down, *, tb, fc):
    e, d, f = w_gate.shape
    rpt = d // LANES
    n_tok = h2t.shape[0] // rpt
    n_blocks = block_nv.shape[0]
    assert n_tok & (n_tok - 1) == 0, "token count must be a power of two (slot -> token by masking)"
    assert d == f, "expert width equals model width in this block"
    wspec = lambda shape: pl.BlockSpec((1,) + shape, lambda i, be, nv: (be[i], 0, 0))
    return pl.pallas_call(
        functools.partial(_moe_kernel, tb=tb, fc=fc, n_blocks=n_blocks, n_tok=n_tok),
        grid_spec=pltpu.PrefetchScalarGridSpec(
            num_scalar_prefetch=2,
            grid=(n_blocks,),
            in_specs=[pl.BlockSpec((1, 1, tb), lambda i, be, nv: (i, 0, 0)),
                      pl.BlockSpec((1, 1, tb), lambda i, be, nv: (jnp.minimum(i + 1, n_blocks - 1), 0, 0)),
                      pl.BlockSpec(memory_space=pl.ANY),
                      wspec((d, f)), wspec((1, f)), wspec((d, f)), wspec((1, f)),
                      wspec((f, d)), wspec((1, d))],
            out_specs=pl.BlockSpec(memory_space=pl.ANY),
            scratch_shapes=[pltpu.VMEM((2, tb * rpt, LANES), F32), pltpu.VMEM((2, tb * rpt, LANES), F32),
                            pltpu.SMEM((tb,), jnp.int32), pltpu.SMEM((tb,), jnp.int32),
                            pltpu.VMEM((d, f), BF16), pltpu.VMEM((d, f), BF16), pltpu.VMEM((f, d), BF16),
                            pltpu.SemaphoreType.DMA((2,)), pltpu.SemaphoreType.DMA((2,)),
                            pltpu.SemaphoreType.DMA]),
        out_shape=jax.ShapeDtypeStruct((TOP_K * n_tok * rpt, LANES), F32),
        compiler_params=pltpu.CompilerParams(dimension_semantics=("arbitrary",),
                                             vmem_limit_bytes=VMEM_LIMIT),
        name="moe",
    )(block_e, block_nv, inv, inv, h2t, w_gate, b_gate.reshape(e, 1, f), w_up, b_up.reshape(e, 1, f),
      w_down, b_down.reshape(e, 1, d))


def _combine_kernel(ys_ref, x1_ref, tw_ref, mod_ref, fg_ref, o_ref):
    tw = tw_ref[...]
    tc, d = x1_ref.shape
    rpt = d // LANES

    def expert_rows(k):
        return jnp.concatenate([ys_ref[k, pl.ds(j, tc, stride=rpt), :] for j in range(rpt)], axis=1)

    moe = tw[:, 0:1] * expert_rows(0)
    for k in range(1, TOP_K):
        moe = moe + tw[:, k:k + 1] * expert_rows(k)
    xo = x1_ref[...] + mod_ref[0, 5:6, :] * moe
    o_ref[...] = _rms(xo) * fg_ref[...]


def _combine(ys4, x1, top_w, mod3, final_g, *, tc, seq):
    n_tok, d = x1.shape
    tiles_per_batch = seq // tc
    return pl.pallas_call(
        _combine_kernel,
        grid=(n_tok // tc,),
        in_specs=[pl.BlockSpec((TOP_K, tc * d // LANES, LANES), lambda i: (0, i, 0)),
                  pl.BlockSpec((tc, d), lambda i: (i, 0)),
                  pl.BlockSpec((tc, TOP_K), lambda i: (i, 0)),
                  pl.BlockSpec((1, 6, d), lambda i: (i // tiles_per_batch, 0, 0)),
                  pl.BlockSpec((1, d), lambda i: (0, 0))],
        out_specs=pl.BlockSpec((tc, d), lambda i: (i, 0)),
        out_shape=jax.ShapeDtypeStruct((n_tok, d), F32),
        compiler_params=pltpu.CompilerParams(dimension_semantics=("arbitrary",),
                                             vmem_limit_bytes=VMEM_LIMIT),
        name="combine",
    )(ys4, x1, top_w, mod3, final_g)


def _forward(x, c, ada_w, ada_b, norm1_g, w_in, lam_re, lam_im, log_dt, b_re, b_im, c_re, c_im, d_skip,
             w_glu, b_glu, conv_w, conv_b, ln_g, ln_b, out_norm_g, w_out, norm2_g, router_w, router_b,
             w_gate, b_gate, w_up, b_up, w_down, b_down, final_g, *, tm, rc, tp, tb, fc, tc):
    bsz, seq, d = x.shape
    n_tok = bsz * seq
    cpb = seq // CHUNK
    n_chunks = n_tok // CHUNK
    row = lambda v: v.reshape(1, -1).astype(F32)

    mod3 = _mod(c, ada_w, ada_b).reshape(bsz, 6, d)

    wu_t = w_in[:, :D_SSM].T.astype(BF16)
    wvg = w_in[:, D_SSM:].astype(BF16)
    cw = jnp.zeros((CONV_HALO, D_CONV), F32).at[:CONV_WIDTH].set(conv_w.astype(F32))
    ut, nconv = _inproj(x, mod3, row(norm1_g), wu_t, wvg, cw, row(conv_b), row(ln_g), row(ln_b),
                        row(out_norm_g[D_SSM:]), tm=tm, rc=rc)

    klag, wst, vout, pq = _ssm_params(lam_re, lam_im, log_dt, b_re, b_im, c_re, c_im, cpb)
    dsk = jnp.broadcast_to(d_skip.astype(F32).reshape(N_GROUPS, SSM_GROUP, 1, 1),
                           (N_GROUPS, SSM_GROUP, 1, CHUNK))
    y4 = _ssm(ut.reshape(N_GROUPS, SSM_GROUP, n_chunks, CHUNK), klag, wst, vout, pq, dsk, cpb=cpb)
    yt = y4.reshape(D_SSM, n_chunks, CHUNK)

    rw = router_w.astype(F32)
    rw_hi = rw.astype(BF16)
    rw_split = jnp.concatenate([rw_hi, (rw - rw_hi.astype(F32)).astype(BF16)], axis=1)
    x1, h2, top_idx, top_w = _mix(
        x, yt, nconv, mod3, w_glu.T.astype(BF16), b_glu.reshape(D_SSM, 1).astype(F32),
        out_norm_g[:D_SSM].reshape(D_SSM, 1).astype(F32), w_out[:D_SSM].astype(BF16),
        w_out[D_SSM:].astype(BF16), row(norm2_g), rw_split, row(router_b), tm=tm)

    rank, counts = _rank(top_idx, tp=tp)
    counts = counts[0]
    padded = (counts + tb - 1) // tb * tb
    pend = jnp.cumsum(padded)
    pstart = pend - padded
    dest = (pstart[top_idx] + rank).reshape(-1)
    n_blocks = (n_tok * TOP_K) // tb + N_EXPERTS
    blk_start = jnp.arange(n_blocks, dtype=jnp.int32) * tb
    block_e = jnp.minimum(jnp.sum((blk_start[:, None] >= pend[None, :]).astype(jnp.int32), axis=1), N_EXPERTS - 1)
    block_nv = jnp.clip(pstart[block_e] + counts[block_e] - blk_start, 0, tb).astype(jnp.int32)
    block_nv = jnp.where(blk_start < pend[-1], block_nv, 0)

    slot = (jnp.arange(TOP_K, dtype=jnp.int32)[None, :] * n_tok + jnp.arange(n_tok, dtype=jnp.int32)[:, None])
    inv = jnp.zeros((n_blocks * tb,), jnp.int32).at[dest].set(slot.reshape(-1)).reshape(n_blocks, 1, tb)

    ys4 = _moe(block_e, block_nv, inv, h2, w_gate, b_gate, w_up, b_up, w_down, b_down, tb=tb, fc=fc)
    out = _combine(ys4.reshape(TOP_K, n_tok * d // LANES, LANES), x1.reshape(n_tok, d), top_w, mod3,
                   row(final_g), tc=tc, seq=seq)
    return out.reshape(bsz, seq, d)


def kernel(x, c, ada_w, ada_b, norm1_g, w_in, lam_re, lam_im, log_dt, b_re, b_im, c_re, c_im, d_skip, w_glu, b_glu, conv_w, conv_b, ln_g, ln_b, out_norm_g, w_out, norm2_g, router_w, router_b, w_gate, b_gate, w_up, b_up, w_down, b_down, final_g):
    p = [a[0] for a in (ada_w, ada_b, norm1_g, w_in, lam_re, lam_im, log_dt, b_re, b_im, c_re, c_im, d_skip,
                        w_glu, b_glu, conv_w, conv_b, ln_g, ln_b, out_norm_g, w_out, norm2_g, router_w,
                        router_b, w_gate, b_gate, w_up, b_up, w_down, b_down)]
    return _forward(x, c, *p, final_g, tm=1024, rc=64, tp=512, tb=512, fc=256, tc=256)
```

```python
import functools
import math

import jax
import jax.numpy as jnp
from jax import lax
from jax.experimental import pallas as pl
from jax.experimental.pallas import tpu as pltpu

F32 = jnp.float32
BF16 = jnp.bfloat16
HIGHEST = lax.Precision.HIGHEST

D_MODEL = 1024
D_SSM = 512
D_CONV = 512
SSM_GROUP = 16
N_GROUPS = 32
SSM_STATE = 64
CONV_WIDTH = 31
N_EXPERTS = 32
TOP_K = 4
SWIGLU_ALPHA = 1.702
SWIGLU_LIMIT = 7.0
RMS_EPS = 1e-6
LN_EPS = 1e-5

CHUNK = 128
CONV_HALO = 32
SUBLANES = 8
LANES = 128
ISSUE_UNROLL = 16
VMEM_LIMIT = 56 * 1024 * 1024


def _rms(x, eps=RMS_EPS):
    return x * lax.rsqrt(jnp.mean(x * x, axis=-1, keepdims=True) + eps)


def _mod_kernel(c_ref, w_ref, b_ref, o_ref):
    c = c_ref[...]
    cond = c * jax.nn.sigmoid(c)
    o_ref[...] = jnp.dot(cond, w_ref[...], precision=HIGHEST, preferred_element_type=F32) + b_ref[...]


def _mod(c, ada_w, ada_b):
    bsz, d = c.shape
    n = ada_w.shape[1]
    return pl.pallas_call(
        _mod_kernel,
        grid=(n // d,),
        in_specs=[pl.BlockSpec((bsz, d), lambda j: (0, 0)),
                  pl.BlockSpec((d, d), lambda j: (0, j)),
                  pl.BlockSpec((1, d), lambda j: (0, j))],
        out_specs=pl.BlockSpec((bsz, d), lambda j: (0, j)),
        out_shape=jax.ShapeDtypeStruct((bsz, n), F32),
        name="mod",
    )(c, ada_w, ada_b.reshape(1, n))


def _inproj_kernel(x_ref, mod_ref, g1_ref, wu_ref, wvg_ref, cw_ref, cb_ref, lg_ref, lb_ref, og_ref,
                   ut_ref, nc_ref, zext, zsh, *, tm, rc):
    i = pl.program_id(1)
    x = x_ref[0]
    y = _rms(x) * g1_ref[...]
    h = (y * (1.0 + mod_ref[0, 1:2, :]) + mod_ref[0, 0:1, :]).astype(BF16)
    ut = lax.dot_general(wu_ref[...], h, (((1,), (1,)), ((), ())), preferred_element_type=F32)
    for cc in range(tm // CHUNK):
        ut_ref[:, cc, :] = ut[:, cc * CHUNK:(cc + 1) * CHUNK]
    vg = jnp.dot(h, wvg_ref[...], preferred_element_type=F32)
    z = vg[:, :D_CONV] * jax.nn.sigmoid(vg[:, D_CONV:])

    @pl.when(i == 0)
    def _():
        zext[0:CONV_HALO, :] = jnp.zeros((CONV_HALO, D_CONV), F32)

    zext[CONV_HALO:CONV_HALO + tm, :] = z
    off = CONV_HALO - (CONV_WIDTH - 1)
    span = tm + CONV_HALO - SUBLANES
    for p in range(1, SUBLANES):
        zsh[p - 1, 0:span, :] = zext[p:p + span, :]
    for r0 in range(0, tm, rc):
        acc = jnp.zeros((rc, D_CONV), F32) + cb_ref[...]
        for k in range(CONV_WIDTH):
            p = (off + k) % SUBLANES
            lo = r0 + off + k - p
            tap = zext[lo:lo + rc, :] if p == 0 else zsh[p - 1, lo:lo + rc, :]
            acc = acc + cw_ref[k:k + 1, :] * tap
        mu = jnp.mean(acc, axis=-1, keepdims=True)
        xc = acc - mu
        ln = xc * lax.rsqrt(jnp.mean(xc * xc, axis=-1, keepdims=True) + LN_EPS) * lg_ref[...] + lb_ref[...]
        act = ln * jax.nn.sigmoid(ln)
        nc_ref[0, r0:r0 + rc, :] = (_rms(act) * og_ref[...]).astype(BF16)
    zext[0:CONV_HALO, :] = zext[tm:tm + CONV_HALO, :]


def _inproj(x, mod3, norm1_g, wu_t, wvg, conv_w, conv_b, ln_g, ln_b, og_conv, *, tm, rc):
    bsz, seq, d = x.shape
    nt = seq // tm
    full = lambda shape: pl.BlockSpec(shape, lambda b, i: (0,) * len(shape))
    return pl.pallas_call(
        functools.partial(_inproj_kernel, tm=tm, rc=rc),
        grid=(bsz, nt),
        in_specs=[pl.BlockSpec((1, tm, d), lambda b, i: (b, i, 0)),
                  pl.BlockSpec((1, 6, d), lambda b, i: (b, 0, 0)),
                  full((1, d)), full((D_SSM, d)), full((d, 2 * D_CONV)),
                  full((CONV_HALO, D_CONV)), full((1, D_CONV)), full((1, D_CONV)), full((1, D_CONV)),
                  full((1, D_CONV))],
        out_specs=[pl.BlockSpec((D_SSM, tm // CHUNK, CHUNK), lambda b, i: (0, b * nt + i, 0)),
                   pl.BlockSpec((1, tm, D_CONV), lambda b, i: (b, i, 0))],
        out_shape=[jax.ShapeDtypeStruct((D_SSM, bsz * seq // CHUNK, CHUNK), F32),
                   jax.ShapeDtypeStruct((bsz, seq, D_CONV), BF16)],
        scratch_shapes=[pltpu.VMEM((tm + CONV_HALO, D_CONV), F32),
                        pltpu.VMEM((SUBLANES - 1, tm + CONV_HALO, D_CONV), F32)],
        compiler_params=pltpu.CompilerParams(dimension_semantics=("arbitrary", "arbitrary"),
                                             vmem_limit_bytes=VMEM_LIMIT),
        name="inproj",
    )(x, mod3, norm1_g, wu_t, wvg, conv_w, conv_b, ln_g, ln_b, og_conv)


def _ssm_kernel(u_ref, klag_ref, w_ref, v_ref, pq_ref, dsk_ref, y_ref, toep, *, cpb):
    nc = u_ref.shape[2]
    s_idx = lax.broadcasted_iota(jnp.int32, (CHUNK, CHUNK), 0)
    j_idx = lax.broadcasted_iota(jnp.int32, (CHUNK, CHUNK), 1)
    causal = j_idx >= s_idx

    def expand(hp, carry):
        kv = klag_ref[0, hp]
        blocks = []
        for h in range(SSM_GROUP):
            rows = jnp.broadcast_to(kv[h:h + 1, :], (CHUNK, CHUNK))
            skew = pltpu.roll(rows, 0, 1, stride=1, stride_axis=0)
            blocks.append(jnp.where(causal, skew, 0.0).astype(BF16))
        toep[pl.ds(pl.multiple_of(hp * CHUNK, CHUNK), CHUNK), :] = jnp.concatenate(blocks, axis=1)
        return carry

    lax.fori_loop(0, SSM_GROUP, expand, 0)

    us = [u_ref[0, h] for h in range(SSM_GROUP)]
    xcat = jnp.concatenate([u.astype(BF16) for u in us], axis=1)
    acc = jnp.dot(xcat, toep[...], preferred_element_type=F32)
    st = jnp.dot(xcat, w_ref[0], preferred_element_type=F32)
    row = lax.broadcasted_iota(jnp.int32, (nc, 2 * SSM_STATE), 0) % cpb
    d = 1
    step = 0
    while d < cpb:
        sh = jnp.where(row >= d, pltpu.roll(st, d, axis=0), 0.0)
        st = st + pq_ref[0, step, 0:1, :] * sh + pq_ref[0, step, 1:2, :] * pltpu.roll(sh, SSM_STATE, axis=1)
        d *= 2
        step += 1
    prev = jnp.where(row >= 1, pltpu.roll(st, 1, axis=0), 0.0)
    acc = acc + jnp.dot(prev.astype(BF16), v_ref[0], preferred_element_type=F32)
    for h in range(SSM_GROUP):
        y_ref[0, h] = acc[:, h * CHUNK:(h + 1) * CHUNK] + dsk_ref[0, h] * us[h]


def _ssm(u4, klag, wst, vout, pq, dsk, *, cpb):
    g, hh, nc, t = u4.shape
    blk = lambda shape: pl.BlockSpec((1,) + shape, lambda i: (i,) + (0,) * len(shape))
    return pl.pallas_call(
        functools.partial(_ssm_kernel, cpb=cpb),
        grid=(g,),
        in_specs=[blk((hh, nc, t)), blk(klag.shape[1:]), blk(wst.shape[1:]), blk(vout.shape[1:]),
                  blk(pq.shape[1:]), blk(dsk.shape[1:])],
        out_specs=blk((hh, nc, t)),
        out_shape=jax.ShapeDtypeStruct(u4.shape, F32),
        scratch_shapes=[pltpu.VMEM((hh * t, hh * t), BF16)],
        compiler_params=pltpu.CompilerParams(dimension_semantics=("arbitrary",),
                                             vmem_limit_bytes=VMEM_LIMIT),
        name="ssm",
    )(u4, klag, wst, vout, pq, dsk)


def _ssm_params(lam_re, lam_im, log_dt, b_re, b_im, c_re, c_im, cpb):
    lr, li = lam_re.astype(F32), lam_im.astype(F32)
    dt = jnp.exp(log_dt.astype(F32))[:, None]
    mag = jnp.exp(lr * dt)
    ab_re = mag * jnp.cos(li * dt)
    ab_im = mag * jnp.sin(li * dt)
    den = lr * lr + li * li
    nr = ab_re - 1.0
    q_re = (nr * lr + ab_im * li) / den
    q_im = (ab_im * lr - nr * li) / den
    br, bi = b_re.astype(F32), b_im.astype(F32)
    bb_re = q_re[..., None] * br - q_im[..., None] * bi
    bb_im = q_re[..., None] * bi + q_im[..., None] * br
    cr, ci = c_re.astype(F32), c_im.astype(F32)

    pr, pi = jnp.ones((1,) + ab_re.shape, F32), jnp.zeros((1,) + ab_re.shape, F32)
    cur_r, cur_i = ab_re, ab_im
    while pr.shape[0] < CHUNK:
        pr, pi = (jnp.concatenate([pr, pr * cur_r - pi * cur_i], axis=0),
                  jnp.concatenate([pi, pr * cur_i + pi * cur_r], axis=0))
        cur_r, cur_i = cur_r * cur_r - cur_i * cur_i, 2.0 * cur_r * cur_i
    cp_re = cr[None] * pr[:, :, None, :] - ci[None] * pi[:, :, None, :]
    cp_im = cr[None] * pi[:, :, None, :] + ci[None] * pr[:, :, None, :]
    klag = (jnp.einsum('lghp,gpk->gkhl', cp_re, bb_re, precision=HIGHEST)
            - jnp.einsum('lghp,gpk->gkhl', cp_im, bb_im, precision=HIGHEST))
    rr, ri = pr[::-1], pi[::-1]
    w_re = rr[..., None] * bb_re[None] - ri[..., None] * bb_im[None]
    w_im = rr[..., None] * bb_im[None] + ri[..., None] * bb_re[None]
    wst = jnp.concatenate([w_re, w_im], axis=2).transpose(1, 3, 0, 2)
    wst = wst.reshape(N_GROUPS, SSM_GROUP * CHUNK, 2 * SSM_STATE).astype(BF16)
    p1r, p1i = pr * ab_re - pi * ab_im, pr * ab_im + pi * ab_re
    v_re = cr[None] * p1r[:, :, None, :] - ci[None] * p1i[:, :, None, :]
    v_im = cr[None] * p1i[:, :, None, :] + ci[None] * p1r[:, :, None, :]
    vout = jnp.concatenate([v_re, -v_im], axis=3).transpose(1, 3, 2, 0)
    vout = vout.reshape(N_GROUPS, 2 * SSM_STATE, SSM_GROUP * CHUNK).astype(BF16)
    tabs = []
    d = 1
    while d < cpb or not tabs:
        tabs.append(jnp.stack([jnp.concatenate([cur_r, cur_r], axis=1),
                               jnp.concatenate([-cur_i, cur_i], axis=1)], axis=1))
        cur_r, cur_i = cur_r * cur_r - cur_i * cur_i, 2.0 * cur_r * cur_i
        d *= 2
    pq = jnp.stack(tabs, axis=1)
    return klag, wst, vout, pq


def _mix_kernel(x_ref, yt_ref, nc_ref, mod_ref, wglu_ref, bglu_ref, ogs_ref, wot_ref, wob_ref, g2_ref,
                rw_ref, rb_ref, x1_ref, h2_ref, idx_ref, tw_ref):
    yt = jnp.concatenate([yt_ref[:, cc, :] for cc in range(yt_ref.shape[1])], axis=1)
    yg = 0.5 * yt * (1.0 + jnp.tanh(math.sqrt(2.0 / math.pi) * (yt + 0.044715 * (yt * yt * yt))))
    gate = jnp.dot(wglu_ref[...], yg.astype(BF16), preferred_element_type=F32) + bglu_ref[...]
    y2 = yg * jax.nn.sigmoid(gate)
    ms = jnp.mean(y2 * y2, axis=0, keepdims=True)
    ns = (y2 * lax.rsqrt(ms + RMS_EPS) * ogs_ref[...]).astype(BF16)
    o = (lax.dot_general(ns, wot_ref[...], (((0,), (0,)), ((), ())), preferred_element_type=F32)
         + jnp.dot(nc_ref[0], wob_ref[...], preferred_element_type=F32))
    x1 = x_ref[0] + mod_ref[0, 2:3, :] * o
    x1_ref[0] = x1
    h2 = _rms(x1) * g2_ref[...] * (1.0 + mod_ref[0, 4:5, :]) + mod_ref[0, 3:4, :]
    rpt = h2.shape[1] // LANES
    for j in range(rpt):
        h2_ref[pl.ds(j, h2.shape[0], stride=rpt), :] = h2[:, j * LANES:(j + 1) * LANES]
    nt_dims = (((1,), (1,)), ((), ()))
    h2_hi = h2.astype(BF16)
    h2_lo = (h2 - h2_hi.astype(F32)).astype(BF16)
    l_hi = lax.dot_general(rw_ref[...], h2_hi, nt_dims, preferred_element_type=F32)
    l_lo = lax.dot_general(rw_ref[:N_EXPERTS, :], h2_lo, nt_dims, preferred_element_type=F32)
    logits = l_hi[:N_EXPERTS, :] + (l_hi[N_EXPERTS:, :] + l_lo) + rb_ref[...]
    expert = lax.broadcasted_iota(jnp.int32, logits.shape, 0)
    vals, idxs = [], []
    for _ in range(TOP_K):
        m = jnp.max(logits, axis=0, keepdims=True)
        sel = jnp.min(jnp.where(logits == m, expert, N_EXPERTS), axis=0, keepdims=True)
        vals.append(m)
        idxs.append(sel)
        logits = jnp.where(expert == sel, -jnp.inf, logits)
    es = [jnp.exp(v - vals[0]) for v in vals]
    tot = es[0] + es[1] + es[2] + es[3]
    idx_ref[...] = jnp.concatenate(idxs, axis=0)
    tw_rows = jnp.concatenate([e / tot for e in es] + [jnp.zeros((LANES - TOP_K, h2.shape[0]), F32)], axis=0)
    tw_ref[...] = tw_rows.T[:, :TOP_K]


def _mix(x, yt, nconv, mod3, wglu_t, bglu, og_ssm, wo_top, wo_bot, norm2_g, router_w, router_b, *, tm):
    bsz, seq, d = x.shape
    nt = seq // tm
    n_tok = bsz * seq
    full = lambda shape: pl.BlockSpec(shape, lambda b, i: (0,) * len(shape))
    return pl.pallas_call(
        _mix_kernel,
        grid=(bsz, nt),
        in_specs=[pl.BlockSpec((1, tm, d), lambda b, i: (b, i, 0)),
                  pl.BlockSpec((D_SSM, tm // CHUNK, CHUNK), lambda b, i: (0, b * nt + i, 0)),
                  pl.BlockSpec((1, tm, D_CONV), lambda b, i: (b, i, 0)),
                  pl.BlockSpec((1, 6, d), lambda b, i: (b, 0, 0)),
                  full((D_SSM, D_SSM)), full((D_SSM, 1)), full((D_SSM, 1)),
                  full((D_SSM, d)), full((D_CONV, d)), full((1, d)),
                  full((2 * N_EXPERTS, d)), full((N_EXPERTS, 1))],
        out_specs=[pl.BlockSpec((1, tm, d), lambda b, i: (b, i, 0)),
                   pl.BlockSpec((tm * d // LANES, LANES), lambda b, i: (b * nt + i, 0)),
                   pl.BlockSpec((TOP_K, tm), lambda b, i: (0, b * nt + i)),
                   pl.BlockSpec((tm, TOP_K), lambda b, i: (b * nt + i, 0))],
        out_shape=[jax.ShapeDtypeStruct((bsz, seq, d), F32),
                   jax.ShapeDtypeStruct((n_tok * d // LANES, LANES), F32),
                   jax.ShapeDtypeStruct((TOP_K, n_tok), jnp.int32),
                   jax.ShapeDtypeStruct((n_tok, TOP_K), F32)],
        compiler_params=pltpu.CompilerParams(dimension_semantics=("arbitrary", "arbitrary"),
                                             vmem_limit_bytes=VMEM_LIMIT),
        name="mix",
    )(x, yt, nconv, mod3, wglu_t, bglu, og_ssm, wo_top, wo_bot, norm2_g, router_w, router_b)


def _rank_kernel(idx_ref, rank_ref, cnt_ref, carry):
    i = pl.program_id(0)
    tp = idx_ref.shape[1]

    @pl.when(i == 0)
    def _():
        carry[...] = jnp.zeros_like(carry)

    idx = idx_ref[...]
    expert = lax.broadcasted_iota(jnp.int32, (N_EXPERTS, tp), 0)
    hot = jnp.zeros((N_EXPERTS, tp), F32)
    for k in range(TOP_K):
        hot = hot + jnp.where(expert == idx[k:k + 1, :], 1.0, 0.0)
    r = lax.broadcasted_iota(jnp.int32, (tp, tp), 0)
    c = lax.broadcasted_iota(jnp.int32, (tp, tp), 1)
    earlier = jnp.where(r < c, 1.0, 0.0).astype(BF16)
    excl = jnp.dot(hot.astype(BF16), earlier, preferred_element_type=F32) + carry[...]
    ranks = [jnp.sum(jnp.where(expert == idx[k:k + 1, :], excl, 0.0), axis=0, keepdims=True)
             for k in range(TOP_K)]
    rank_ref[...] = jnp.concatenate(ranks, axis=0).astype(jnp.int32)
    carry[...] = carry[...] + jnp.sum(hot, axis=1, keepdims=True)
    cnt_ref[...] = carry[...].astype(jnp.int32)


def _rank(top_idx, *, tp):
    n_tok = top_idx.shape[1]
    return pl.pallas_call(
        _rank_kernel,
        grid=(n_tok // tp,),
        in_specs=[pl.BlockSpec((TOP_K, tp), lambda i: (0, i))],
        out_specs=[pl.BlockSpec((TOP_K, tp), lambda i: (0, i)),
                   pl.BlockSpec((N_EXPERTS, 1), lambda i: (0, 0))],
        out_shape=[jax.ShapeDtypeStruct((TOP_K, n_tok), jnp.int32),
                   jax.ShapeDtypeStruct((N_EXPERTS, 1), jnp.int32)],
        scratch_shapes=[pltpu.VMEM((N_EXPERTS, 1), F32)],
        compiler_params=pltpu.CompilerParams(dimension_semantics=("arbitrary",)),
        name="rank",
    )(top_idx)


def _moe_kernel(be_ref, nv_ref, inv_ref, h_ref, wg_ref, bg_ref, wu_ref, bu_ref, wd_ref, bd_ref,
                ys_ref, xbuf, ybuf, slots0, slots1, wg_s, wu_s, wd_s, gsem, ssem, isem,
                *, tb, fc, n_blocks, n_tok):
    i = pl.program_id(0)
    b = i % 2
    nv = nv_ref[i]
    f = wg_s.shape[1]
    d = f
    rpt = d // LANES
    slot_bufs = (slots0, slots1)

    def gather_row(buf, r):
        tok = slot_bufs[buf][r] & (n_tok - 1)
        return pltpu.make_async_copy(h_ref.at[pl.ds(pl.multiple_of(tok * rpt, rpt), rpt), :],
                                     xbuf.at[buf, pl.ds(pl.multiple_of(r * rpt, rpt), rpt), :], gsem.at[buf])

    def scatter_row(buf, r):
        return pltpu.make_async_copy(ybuf.at[buf, pl.ds(pl.multiple_of(r * rpt, rpt), rpt), :],
                                     ys_ref.at[pl.ds(pl.multiple_of(slot_bufs[buf][r] * rpt, rpt), rpt), :],
                                     ssem.at[buf])

    def gather_all(buf):
        return pltpu.make_async_copy(h_ref.at[pl.ds(0, tb * rpt), :], xbuf.at[buf], gsem.at[buf])

    def scatter_all(buf):
        return pltpu.make_async_copy(ybuf.at[buf], ys_ref.at[pl.ds(0, tb * rpt), :], ssem.at[buf])

    def on_buffer(which, fn):
        for buf in (0, 1):
            @pl.when(which == buf)
            def _():
                fn(buf)

    def start_rows(n, row_copy):
        @pl.when(n == tb)
        def _():
            def body(g, carry):
                for u in range(ISSUE_UNROLL):
                    row_copy(g * ISSUE_UNROLL + u).start()
                return carry
            lax.fori_loop(0, tb // ISSUE_UNROLL, body, 0)

        @pl.when(n < tb)
        def _():
            def body(r, carry):
                row_copy(r).start()
                return carry
            lax.fori_loop(0, n, body, 0)

    def wait_rows(n, all_copy, row_copy):
        @pl.when(n == tb)
        def _():
            all_copy.wait()

        @pl.when(n < tb)
        def _():
            def body(r, carry):
                row_copy.wait()
                return carry
            lax.fori_loop(0, n, body, 0)

    def slots_copy(blk, buf):
        return pltpu.make_async_copy(inv_ref.at[jnp.minimum(blk, n_blocks - 1), 0], slot_bufs[buf], isem.at[buf])

    def nv_of(blk):
        return jnp.where((blk >= 0) & (blk < n_blocks), nv_ref[jnp.clip(blk, 0, n_blocks - 1)], 0)

    nv_next = nv_of(i + 1)

    @pl.when(i == 0)
    def _():
        xbuf[...] = jnp.zeros_like(xbuf)
        slots_copy(0, 0).start()
        slots_copy(0, 0).wait()
        start_rows(nv, lambda r: gather_row(0, r))

        @pl.when(nv_next > 0)
        def _():
            slots_copy(1, 1).start()

    def wait_inputs(buf):
        wait_rows(nv, gather_all(buf), gather_row(buf, 0))
        wait_rows(nv_of(i - 2), scatter_all(buf), scatter_row(buf, 0))

    on_buffer(b, wait_inputs)

    def prefetch_next(buf):
        slots_copy(i + 1, buf).wait()
        start_rows(nv_next, lambda r: gather_row(buf, r))

    @pl.when(nv_next > 0)
    def _():
        on_buffer(1 - b, prefetch_next)

    @pl.when(nv > 0)
    def _():
        @pl.when((i == 0) | (be_ref[i] != be_ref[jnp.maximum(i - 1, 0)]))
        def _():
            wg_s[...] = wg_ref[0].astype(BF16)
            wu_s[...] = wu_ref[0].astype(BF16)
            wd_s[...] = wd_ref[0].astype(BF16)

        xb = jnp.concatenate([xbuf[b, pl.ds(j, tb, stride=rpt), :] for j in range(rpt)], axis=1).astype(BF16)
        y = None
        for j in range(f // fc):
            cols = slice(j * fc, (j + 1) * fc)
            gate = jnp.minimum(jnp.dot(xb, wg_s[:, cols], preferred_element_type=F32) + bg_ref[0, :, cols],
                               SWIGLU_LIMIT)
            up = jnp.clip(jnp.dot(xb, wu_s[:, cols], preferred_element_type=F32) + bu_ref[0, :, cols],
                          -SWIGLU_LIMIT, SWIGLU_LIMIT)
            act = ((up + 1.0) * (gate * jax.nn.sigmoid(SWIGLU_ALPHA * gate))).astype(BF16)
            part = jnp.dot(act, wd_s[cols, :], preferred_element_type=F32)
            y = part if y is None else y + part
        y = y + bd_ref[0]
        for j in range(rpt):
            ybuf[b, pl.ds(j, tb, stride=rpt), :] = y[:, j * LANES:(j + 1) * LANES]

    def send_outputs(buf):
        start_rows(nv, lambda r: scatter_row(buf, r))

        @pl.when(nv_of(i + 2) > 0)
        def _():
            slots_copy(i + 2, buf).start()

    on_buffer(b, send_outputs)

    @pl.when(i == n_blocks - 1)
    def _():
        on_buffer(1 - b, lambda buf: wait_rows(nv_of(i - 1), scatter_all(buf), scatter_row(buf, 0)))
        on_buffer(b, lambda buf: wait_rows(nv, scatter_all(buf), scatter_row(buf, 0)))


def _moe(block_e, block_nv, inv, h2t, w_gate, b_gate, w_up, b_up, w_down, b_down, *, tb, fc):
    e, d, f = w_gate.shape
    rpt = d // LANES
    n_tok = h2t.shape[0] // rpt
    n_blocks = block_nv.shape[0]
    assert n_tok & (n_tok - 1) == 0, "token count must be a power of two (slot -> token by masking)"
    assert d == f, "expert width equals model width in this block"
    wspec = lambda shape: pl.BlockSpec((1,) + shape, lambda i, be, nv: (be[i], 0, 0))
    return pl.pallas_call(
        functools.partial(_moe_kernel, tb=tb, fc=fc, n_blocks=n_blocks, n_tok=n_tok),
        grid_spec=pltpu.PrefetchScalarGridSpec(
            num_scalar_prefetch=2,
            grid=(n_blocks,),
            in_specs=[pl.BlockSpec(memory_space=pl.ANY), pl.BlockSpec(memory_space=pl.ANY),
                      wspec((d, f)), wspec((1, f)), wspec((d, f)), wspec((1, f)),
                      wspec((f, d)), wspec((1, d))],
            out_specs=pl.BlockSpec(memory_space=pl.ANY),
            scratch_shapes=[pltpu.VMEM((2, tb * rpt, LANES), F32), pltpu.VMEM((2, tb * rpt, LANES), F32),
                            pltpu.SMEM((tb,), jnp.int32), pltpu.SMEM((tb,), jnp.int32),
                            pltpu.VMEM((d, f), BF16), pltpu.VMEM((d, f), BF16), pltpu.VMEM((f, d), BF16),
                            pltpu.SemaphoreType.DMA((2,)), pltpu.SemaphoreType.DMA((2,)),
                            pltpu.SemaphoreType.DMA((2,))]),
        out_shape=jax.ShapeDtypeStruct((TOP_K * n_tok * rpt, LANES), F32),
        compiler_params=pltpu.CompilerParams(dimension_semantics=("arbitrary",),
                                             vmem_limit_bytes=VMEM_LIMIT),
        name="moe",
    )(block_e, block_nv, inv, h2t, w_gate, b_gate.reshape(e, 1, f), w_up, b_up.reshape(e, 1, f),
      w_down, b_down.reshape(e, 1, d))


def _combine_kernel(ys_ref, x1_ref, tw_ref, mod_ref, fg_ref, o_ref):
    tw = tw_ref[...]
    tc, d = x1_ref.shape
    rpt = d // LANES

    def expert_rows(k):
        return jnp.concatenate([ys_ref[k, pl.ds(j, tc, stride=rpt), :] for j in range(rpt)], axis=1)

    moe = tw[:, 0:1] * expert_rows(0)
    for k in range(1, TOP_K):
        moe = moe + tw[:, k:k + 1] * expert_rows(k)
    xo = x1_ref[...] + mod_ref[0, 5:6, :] * moe
    o_ref[...] = _rms(xo) * fg_ref[...]


def _combine(ys4, x1, top_w, mod3, final_g, *, tc, seq):
    n_tok, d = x1.shape
    tiles_per_batch = seq // tc
    return pl.pallas_call(
        _combine_kernel,
        grid=(n_tok // tc,),
        in_specs=[pl.BlockSpec((TOP_K, tc * d // LANES, LANES), lambda i: (0, i, 0)),
                  pl.BlockSpec((tc, d), lambda i: (i, 0)),
                  pl.BlockSpec((tc, TOP_K), lambda i: (i, 0)),
                  pl.BlockSpec((1, 6, d), lambda i: (i // tiles_per_batch, 0, 0)),
                  pl.BlockSpec((1, d), lambda i: (0, 0))],
        out_specs=pl.BlockSpec((tc, d), lambda i: (i, 0)),
        out_shape=jax.ShapeDtypeStruct((n_tok, d), F32),
        compiler_params=pltpu.CompilerParams(dimension_semantics=("arbitrary",),
                                             vmem_limit_bytes=VMEM_LIMIT),
        name="combine",
    )(ys4, x1, top_w, mod3, final_g)


def _forward(x, c, ada_w, ada_b, norm1_g, w_in, lam_re, lam_im, log_dt, b_re, b_im, c_re, c_im, d_skip,
             w_glu, b_glu, conv_w, conv_b, ln_g, ln_b, out_norm_g, w_out, norm2_g, router_w, router_b,
             w_gate, b_gate, w_up, b_up, w_down, b_down, final_g, *, tm, rc, tp, tb, fc, tc):
    bsz, seq, d = x.shape
    n_tok = bsz * seq
    cpb = seq // CHUNK
    n_chunks = n_tok // CHUNK
    row = lambda v: v.reshape(1, -1).astype(F32)

    mod3 = _mod(c, ada_w, ada_b).reshape(bsz, 6, d)

    wu_t = w_in[:, :D_SSM].T.astype(BF16)
    wvg = w_in[:, D_SSM:].astype(BF16)
    cw = jnp.zeros((CONV_HALO, D_CONV), F32).at[:CONV_WIDTH].set(conv_w.astype(F32))
    ut, nconv = _inproj(x, mod3, row(norm1_g), wu_t, wvg, cw, row(conv_b), row(ln_g), row(ln_b),
                        row(out_norm_g[D_SSM:]), tm=tm, rc=rc)

    klag, wst, vout, pq = _ssm_params(lam_re, lam_im, log_dt, b_re, b_im, c_re, c_im, cpb)
    dsk = jnp.broadcast_to(d_skip.astype(F32).reshape(N_GROUPS, SSM_GROUP, 1, 1),
                           (N_GROUPS, SSM_GROUP, 1, CHUNK))
    y4 = _ssm(ut.reshape(N_GROUPS, SSM_GROUP, n_chunks, CHUNK), klag, wst, vout, pq, dsk, cpb=cpb)
    yt = y4.reshape(D_SSM, n_chunks, CHUNK)

    rw = router_w.astype(F32)
    rw_hi = rw.astype(BF16)
    rw_split = jnp.concatenate([rw_hi, (rw - rw_hi.astype(F32)).astype(BF16)], axis=1).T
    x1, h2, top_idx, top_w = _mix(
        x, yt, nconv, mod3, w_glu.T.astype(BF16), b_glu.reshape(D_SSM, 1).astype(F32),
        out_norm_g[:D_SSM].reshape(D_SSM, 1).astype(F32), w_out[:D_SSM].astype(BF16),
        w_out[D_SSM:].astype(BF16), row(norm2_g), rw_split, router_b.reshape(N_EXPERTS, 1).astype(F32), tm=tm)

    rank, counts = _rank(top_idx, tp=tp)
    counts = counts[:, 0]
    padded = (counts + tb - 1) // tb * tb
    pend = jnp.cumsum(padded)
    pstart = pend - padded
    dest = (pstart[top_idx] + rank).reshape(-1)
    n_blocks = (n_tok * TOP_K) // tb + N_EXPERTS
    blk_start = jnp.arange(n_blocks, dtype=jnp.int32) * tb
    block_e = jnp.minimum(jnp.sum((blk_start[:, None] >= pend[None, :]).astype(jnp.int32), axis=1), N_EXPERTS - 1)
    block_nv = jnp.clip(pstart[block_e] + counts[block_e] - blk_start, 0, tb).astype(jnp.int32)
    block_nv = jnp.where(blk_start < pend[-1], block_nv, 0)

    inv = jnp.zeros((n_blocks * tb,), jnp.int32).at[dest].set(jnp.arange(n_tok * TOP_K, dtype=jnp.int32))
    inv = inv.reshape(n_blocks, 1, tb)

    ys4 = _moe(block_e, block_nv, inv, h2, w_gate, b_gate, w_up, b_up, w_down, b_down, tb=tb, fc=fc)
    out = _combine(ys4.reshape(TOP_K, n_tok * d // LANES, LANES), x1.reshape(n_tok, d), top_w, mod3,
                   row(final_g), tc=tc, seq=seq)
    return out.reshape(bsz, seq, d)


def kernel(x, c, ada_w, ada_b, norm1_g, w_in, lam_re, lam_im, log_dt, b_re, b_im, c_re, c_im, d_skip, w_glu, b_glu, conv_w, conv_b, ln_g, ln_b, out_norm_g, w_out, norm2_g, router_w, router_b, w_gate, b_gate, w_up, b_up, w_down, b_down, final_g):
    p = [a[0] for a in (ada_w, ada_b, norm1_g, w_in, lam_re, lam_im, log_dt, b_re, b_im, c_re, c_im, d_skip,
                        w_glu, b_glu, conv_w, conv_b, ln_g, ln_b, out_norm_g, w_out, norm2_g, router_w,
                        router_b, w_gate, b_gate, w_up, b_up, w_down, b_down)]
    return _forward(x, c, *p, final_g, tm=1024, rc=64, tp=512, tb=512, fc=256, tc=256)
```

```python
import functools
import math

import jax
import jax.numpy as jnp
from jax import lax
from jax.experimental import pallas as pl
from jax.experimental.pallas import tpu as pltpu

F32 = jnp.float32
BF16 = jnp.bfloat16
HIGHEST = lax.Precision.HIGHEST

D_MODEL = 1024
D_SSM = 512
D_CONV = 512
SSM_GROUP = 16
N_GROUPS = 32
SSM_STATE = 64
CONV_WIDTH = 31
N_EXPERTS = 32
TOP_K = 4
SWIGLU_ALPHA = 1.702
SWIGLU_LIMIT = 7.0
RMS_EPS = 1e-6
LN_EPS = 1e-5

CHUNK = 128
CONV_HALO = 32
SUBLANES = 8
LANES = 128
ISSUE_UNROLL = 16
VMEM_LIMIT = 56 * 1024 * 1024


def _rms(x, eps=RMS_EPS):
    return x * lax.rsqrt(jnp.mean(x * x, axis=-1, keepdims=True) + eps)


def _mod_kernel(c_ref, w_ref, b_ref, o_ref):
    c = c_ref[...]
    cond = c * jax.nn.sigmoid(c)
    o_ref[...] = jnp.dot(cond, w_ref[...], precision=HIGHEST, preferred_element_type=F32) + b_ref[...]


def _mod(c, ada_w, ada_b):
    bsz, d = c.shape
    n = ada_w.shape[1]
    return pl.pallas_call(
        _mod_kernel,
        grid=(n // d,),
        in_specs=[pl.BlockSpec((bsz, d), lambda j: (0, 0)),
                  pl.BlockSpec((d, d), lambda j: (0, j)),
                  pl.BlockSpec((1, d), lambda j: (0, j))],
        out_specs=pl.BlockSpec((bsz, d), lambda j: (0, j)),
        out_shape=jax.ShapeDtypeStruct((bsz, n), F32),
        name="mod",
    )(c, ada_w, ada_b.reshape(1, n))


def _inproj_kernel(x_ref, mod_ref, g1_ref, wu_ref, wvg_ref, cw_ref, cb_ref, lg_ref, lb_ref, og_ref,
                   ut_ref, nc_ref, zext, zsh, *, tm, rc):
    i = pl.program_id(1)
    x = x_ref[0]
    y = _rms(x) * g1_ref[...]
    h = (y * (1.0 + mod_ref[0, 1:2, :]) + mod_ref[0, 0:1, :]).astype(BF16)
    ut = lax.dot_general(wu_ref[...], h, (((1,), (1,)), ((), ())), preferred_element_type=F32)
    for cc in range(tm // CHUNK):
        ut_ref[:, cc, :] = ut[:, cc * CHUNK:(cc + 1) * CHUNK]
    vg = jnp.dot(h, wvg_ref[...], preferred_element_type=F32)
    z = vg[:, :D_CONV] * jax.nn.sigmoid(vg[:, D_CONV:])

    @pl.when(i == 0)
    def _():
        zext[0:CONV_HALO, :] = jnp.zeros((CONV_HALO, D_CONV), F32)

    zext[CONV_HALO:CONV_HALO + tm, :] = z
    off = CONV_HALO - (CONV_WIDTH - 1)
    span = tm + CONV_HALO - SUBLANES
    for p in range(1, SUBLANES):
        zsh[p - 1, 0:span, :] = zext[p:p + span, :]
    for r0 in range(0, tm, rc):
        acc = jnp.zeros((rc, D_CONV), F32) + cb_ref[...]
        for k in range(CONV_WIDTH):
            p = (off + k) % SUBLANES
            lo = r0 + off + k - p
            tap = zext[lo:lo + rc, :] if p == 0 else zsh[p - 1, lo:lo + rc, :]
            acc = acc + cw_ref[k:k + 1, :] * tap
        mu = jnp.mean(acc, axis=-1, keepdims=True)
        xc = acc - mu
        ln = xc * lax.rsqrt(jnp.mean(xc * xc, axis=-1, keepdims=True) + LN_EPS) * lg_ref[...] + lb_ref[...]
        act = ln * jax.nn.sigmoid(ln)
        nc_ref[0, r0:r0 + rc, :] = (_rms(act) * og_ref[...]).astype(BF16)
    zext[0:CONV_HALO, :] = zext[tm:tm + CONV_HALO, :]


def _inproj(x, mod3, norm1_g, wu_t, wvg, conv_w, conv_b, ln_g, ln_b, og_conv, *, tm, rc):
    bsz, seq, d = x.shape
    nt = seq // tm
    full = lambda shape: pl.BlockSpec(shape, lambda b, i: (0,) * len(shape))
    return pl.pallas_call(
        functools.partial(_inproj_kernel, tm=tm, rc=rc),
        grid=(bsz, nt),
        in_specs=[pl.BlockSpec((1, tm, d), lambda b, i: (b, i, 0)),
                  pl.BlockSpec((1, 6, d), lambda b, i: (b, 0, 0)),
                  full((1, d)), full((D_SSM, d)), full((d, 2 * D_CONV)),
                  full((CONV_HALO, D_CONV)), full((1, D_CONV)), full((1, D_CONV)), full((1, D_CONV)),
                  full((1, D_CONV))],
        out_specs=[pl.BlockSpec((D_SSM, tm // CHUNK, CHUNK), lambda b, i: (0, b * nt + i, 0)),
                   pl.BlockSpec((1, tm, D_CONV), lambda b, i: (b, i, 0))],
        out_shape=[jax.ShapeDtypeStruct((D_SSM, bsz * seq // CHUNK, CHUNK), F32),
                   jax.ShapeDtypeStruct((bsz, seq, D_CONV), BF16)],
        scratch_shapes=[pltpu.VMEM((tm + CONV_HALO, D_CONV), F32),
                        pltpu.VMEM((SUBLANES - 1, tm + CONV_HALO, D_CONV), F32)],
        compiler_params=pltpu.CompilerParams(dimension_semantics=("arbitrary", "arbitrary"),
                                             vmem_limit_bytes=VMEM_LIMIT),
        name="inproj",
    )(x, mod3, norm1_g, wu_t, wvg, conv_w, conv_b, ln_g, ln_b, og_conv)


def _ssm_kernel(u_ref, klag_ref, w_ref, v_ref, pq_ref, dsk_ref, y_ref, toep, *, cpb):
    nc = u_ref.shape[2]
    s_idx = lax.broadcasted_iota(jnp.int32, (CHUNK, CHUNK), 0)
    j_idx = lax.broadcasted_iota(jnp.int32, (CHUNK, CHUNK), 1)
    causal = j_idx >= s_idx

    def expand(hp, carry):
        kv = klag_ref[0, hp]
        blocks = []
        for h in range(SSM_GROUP):
            rows = jnp.broadcast_to(kv[h:h + 1, :], (CHUNK, CHUNK))
            skew = pltpu.roll(rows, 0, 1, stride=1, stride_axis=0)
            blocks.append(jnp.where(causal, skew, 0.0).astype(BF16))
        toep[pl.ds(pl.multiple_of(hp * CHUNK, CHUNK), CHUNK), :] = jnp.concatenate(blocks, axis=1)
        return carry

    lax.fori_loop(0, SSM_GROUP, expand, 0)

    us = [u_ref[0, h] for h in range(SSM_GROUP)]
    xcat = jnp.concatenate([u.astype(BF16) for u in us], axis=1)
    acc = jnp.dot(xcat, toep[...], preferred_element_type=F32)
    st = jnp.dot(xcat, w_ref[0], preferred_element_type=F32)
    row = lax.broadcasted_iota(jnp.int32, (nc, 2 * SSM_STATE), 0) % cpb
    d = 1
    step = 0
    while d < cpb:
        sh = jnp.where(row >= d, pltpu.roll(st, d, axis=0), 0.0)
        st = st + pq_ref[0, step, 0:1, :] * sh + pq_ref[0, step, 1:2, :] * pltpu.roll(sh, SSM_STATE, axis=1)
        d *= 2
        step += 1
    prev = jnp.where(row >= 1, pltpu.roll(st, 1, axis=0), 0.0)
    acc = acc + jnp.dot(prev.astype(BF16), v_ref[0], preferred_element_type=F32)
    for h in range(SSM_GROUP):
        y_ref[0, h] = acc[:, h * CHUNK:(h + 1) * CHUNK] + dsk_ref[0, h] * us[h]


def _ssm(u4, klag, wst, vout, pq, dsk, *, cpb):
    g, hh, nc, t = u4.shape
    blk = lambda shape: pl.BlockSpec((1,) + shape, lambda i: (i,) + (0,) * len(shape))
    return pl.pallas_call(
        functools.partial(_ssm_kernel, cpb=cpb),
        grid=(g,),
        in_specs=[blk((hh, nc, t)), blk(klag.shape[1:]), blk(wst.shape[1:]), blk(vout.shape[1:]),
                  blk(pq.shape[1:]), blk(dsk.shape[1:])],
        out_specs=blk((hh, nc, t)),
        out_shape=jax.ShapeDtypeStruct(u4.shape, F32),
        scratch_shapes=[pltpu.VMEM((hh * t, hh * t), BF16)],
        compiler_params=pltpu.CompilerParams(dimension_semantics=("arbitrary",),
                                             vmem_limit_bytes=VMEM_LIMIT),
        name="ssm",
    )(u4, klag, wst, vout, pq, dsk)


def _ssm_params(lam_re, lam_im, log_dt, b_re, b_im, c_re, c_im, cpb):
    lr, li = lam_re.astype(F32), lam_im.astype(F32)
    dt = jnp.exp(log_dt.astype(F32))[:, None]
    mag = jnp.exp(lr * dt)
    ab_re = mag * jnp.cos(li * dt)
    ab_im = mag * jnp.sin(li * dt)
    den = lr * lr + li * li
    nr = ab_re - 1.0
    q_re = (nr * lr + ab_im * li) / den
    q_im = (ab_im * lr - nr * li) / den
    br, bi = b_re.astype(F32), b_im.astype(F32)
    bb_re = q_re[..., None] * br - q_im[..., None] * bi
    bb_im = q_re[..., None] * bi + q_im[..., None] * br
    cr, ci = c_re.astype(F32), c_im.astype(F32)

    pr, pi = jnp.ones((1,) + ab_re.shape, F32), jnp.zeros((1,) + ab_re.shape, F32)
    cur_r, cur_i = ab_re, ab_im
    while pr.shape[0] < CHUNK:
        pr, pi = (jnp.concatenate([pr, pr * cur_r - pi * cur_i], axis=0),
                  jnp.concatenate([pi, pr * cur_i + pi * cur_r], axis=0))
        cur_r, cur_i = cur_r * cur_r - cur_i * cur_i, 2.0 * cur_r * cur_i
    cp_re = cr[None] * pr[:, :, None, :] - ci[None] * pi[:, :, None, :]
    cp_im = cr[None] * pi[:, :, None, :] + ci[None] * pr[:, :, None, :]
    klag = (jnp.einsum('lghp,gpk->gkhl', cp_re, bb_re, precision=HIGHEST)
            - jnp.einsum('lghp,gpk->gkhl', cp_im, bb_im, precision=HIGHEST))
    rr, ri = pr[::-1], pi[::-1]
    w_re = rr[..., None] * bb_re[None] - ri[..., None] * bb_im[None]
    w_im = rr[..., None] * bb_im[None] + ri[..., None] * bb_re[None]
    wst = jnp.concatenate([w_re, w_im], axis=2).transpose(1, 3, 0, 2)
    wst = wst.reshape(N_GROUPS, SSM_GROUP * CHUNK, 2 * SSM_STATE).astype(BF16)
    p1r, p1i = pr * ab_re - pi * ab_im, pr * ab_im + pi * ab_re
    v_re = cr[None] * p1r[:, :, None, :] - ci[None] * p1i[:, :, None, :]
    v_im = cr[None] * p1i[:, :, None, :] + ci[None] * p1r[:, :, None, :]
    vout = jnp.concatenate([v_re, -v_im], axis=3).transpose(1, 3, 2, 0)
    vout = vout.reshape(N_GROUPS, 2 * SSM_STATE, SSM_GROUP * CHUNK).astype(BF16)
    tabs = []
    d = 1
    while d < cpb or not tabs:
        tabs.append(jnp.stack([jnp.concatenate([cur_r, cur_r], axis=1),
                               jnp.concatenate([-cur_i, cur_i], axis=1)], axis=1))
        cur_r, cur_i = cur_r * cur_r - cur_i * cur_i, 2.0 * cur_r * cur_i
        d *= 2
    pq = jnp.stack(tabs, axis=1)
    return klag, wst, vout, pq


def _mix_kernel(x_ref, yt_ref, nc_ref, mod_ref, wglu_ref, bglu_ref, ogs_ref, wot_ref, wob_ref, g2_ref,
                rw_ref, rb_ref, x1_ref, h2_ref, idx_ref, tw_ref):
    yt = jnp.concatenate([yt_ref[:, cc, :] for cc in range(yt_ref.shape[1])], axis=1)
    yg = 0.5 * yt * (1.0 + jnp.tanh(math.sqrt(2.0 / math.pi) * (yt + 0.044715 * (yt * yt * yt))))
    gate = jnp.dot(wglu_ref[...], yg.astype(BF16), preferred_element_type=F32) + bglu_ref[...]
    y2 = yg * jax.nn.sigmoid(gate)
    ms = jnp.mean(y2 * y2, axis=0, keepdims=True)
    ns = (y2 * lax.rsqrt(ms + RMS_EPS) * ogs_ref[...]).astype(BF16)
    o = (lax.dot_general(ns, wot_ref[...], (((0,), (0,)), ((), ())), preferred_element_type=F32)
         + jnp.dot(nc_ref[0], wob_ref[...], preferred_element_type=F32))
    x1 = x_ref[0] + mod_ref[0, 2:3, :] * o
    x1_ref[0] = x1
    h2 = _rms(x1) * g2_ref[...] * (1.0 + mod_ref[0, 4:5, :]) + mod_ref[0, 3:4, :]
    rpt = h2.shape[1] // LANES
    for j in range(rpt):
        h2_ref[pl.ds(j, h2.shape[0], stride=rpt), :] = h2[:, j * LANES:(j + 1) * LANES]
    nt_dims = (((1,), (1,)), ((), ()))
    h2_hi = h2.astype(BF16)
    h2_lo = (h2 - h2_hi.astype(F32)).astype(BF16)
    l_hi = lax.dot_general(rw_ref[...], h2_hi, nt_dims, preferred_element_type=F32)
    l_lo = lax.dot_general(rw_ref[:N_EXPERTS, :], h2_lo, nt_dims, preferred_element_type=F32)
    logits = l_hi[:N_EXPERTS, :] + (l_hi[N_EXPERTS:, :] + l_lo) + rb_ref[...]
    expert = lax.broadcasted_iota(jnp.int32, logits.shape, 0)
    vals, idxs = [], []
    for _ in range(TOP_K):
        m = jnp.max(logits, axis=0, keepdims=True)
        sel = jnp.min(jnp.where(logits == m, expert, N_EXPERTS), axis=0, keepdims=True)
        vals.append(m)
        idxs.append(sel)
        logits = jnp.where(expert == sel, -jnp.inf, logits)
    es = [jnp.exp(v - vals[0]) for v in vals]
    tot = es[0] + es[1] + es[2] + es[3]
    idx_ref[...] = jnp.concatenate(idxs, axis=0)
    tw_rows = jnp.concatenate([e / tot for e in es] + [jnp.zeros((LANES - TOP_K, h2.shape[0]), F32)], axis=0)
    tw_ref[...] = tw_rows.T[:, :TOP_K]


def _mix(x, yt, nconv, mod3, wglu_t, bglu, og_ssm, wo_top, wo_bot, norm2_g, router_w, router_b, *, tm):
    bsz, seq, d = x.shape
    nt = seq // tm
    n_tok = bsz * seq
    full = lambda shape: pl.BlockSpec(shape, lambda b, i: (0,) * len(shape))
    return pl.pallas_call(
        _mix_kernel,
        grid=(bsz, nt),
        in_specs=[pl.BlockSpec((1, tm, d), lambda b, i: (b, i, 0)),
                  pl.BlockSpec((D_SSM, tm // CHUNK, CHUNK), lambda b, i: (0, b * nt + i, 0)),
                  pl.BlockSpec((1, tm, D_CONV), lambda b, i: (b, i, 0)),
                  pl.BlockSpec((1, 6, d), lambda b, i: (b, 0, 0)),
                  full((D_SSM, D_SSM)), full((D_SSM, 1)), full((D_SSM, 1)),
                  full((D_SSM, d)), full((D_CONV, d)), full((1, d)),
                  full((2 * N_EXPERTS, d)), full((N_EXPERTS, 1))],
        out_specs=[pl.BlockSpec((1, tm, d), lambda b, i: (b, i, 0)),
                   pl.BlockSpec((tm * d // LANES, LANES), lambda b, i: (b * nt + i, 0)),
                   pl.BlockSpec((TOP_K, tm), lambda b, i: (0, b * nt + i)),
                   pl.BlockSpec((tm, TOP_K), lambda b, i: (b * nt + i, 0))],
        out_shape=[jax.ShapeDtypeStruct((bsz, seq, d), F32),
                   jax.ShapeDtypeStruct((n_tok * d // LANES, LANES), F32),
                   jax.ShapeDtypeStruct((TOP_K, n_tok), jnp.int32),
                   jax.ShapeDtypeStruct((n_tok, TOP_K), F32)],
        compiler_params=pltpu.CompilerParams(dimension_semantics=("arbitrary", "arbitrary"),
                                             vmem_limit_bytes=VMEM_LIMIT),
        name="mix",
    )(x, yt, nconv, mod3, wglu_t, bglu, og_ssm, wo_top, wo_bot, norm2_g, router_w, router_b)


def _moe_kernel(be_ref, nv_ref, inv_ref, h_ref, wg_ref, bg_ref, wu_ref, bu_ref, wd_ref, bd_ref,
                ys_ref, xbuf, ybuf, xb_s, slots, wg_s, wu_s, wd_s, gsem, ssem, isem,
                *, tb, fc, n_blocks, n_tok):
    i = pl.program_id(0)
    b = i % 2
    nv = nv_ref[i]
    f = wg_s.shape[1]
    d = f
    rpt = d // LANES

    def gather_row(buf, r):
        tok = slots[buf * tb + r] & (n_tok - 1)
        return pltpu.make_async_copy(h_ref.at[pl.ds(pl.multiple_of(tok * rpt, rpt), rpt), :],
                                     xbuf.at[buf, pl.ds(pl.multiple_of(r * rpt, rpt), rpt), :], gsem.at[buf])

    def scatter_row(buf, r):
        return pltpu.make_async_copy(ybuf.at[buf, pl.ds(pl.multiple_of(r * rpt, rpt), rpt), :],
                                     ys_ref.at[pl.ds(pl.multiple_of(slots[buf * tb + r] * rpt, rpt), rpt), :],
                                     ssem.at[buf])

    def gather_all(buf):
        return pltpu.make_async_copy(h_ref.at[pl.ds(0, tb * rpt), :], xbuf.at[buf], gsem.at[buf])

    def scatter_all(buf):
        return pltpu.make_async_copy(ybuf.at[buf], ys_ref.at[pl.ds(0, tb * rpt), :], ssem.at[buf])

    def on_buffer(which, fn):
        for buf in (0, 1):
            @pl.when(which == buf)
            def _():
                fn(buf)

    def start_rows(n, row_copy):
        @pl.when(n == tb)
        def _():
            def body(g, carry):
                for u in range(ISSUE_UNROLL):
                    row_copy(g * ISSUE_UNROLL + u).start()
                return carry
            lax.fori_loop(0, tb // ISSUE_UNROLL, body, 0)

        @pl.when(n < tb)
        def _():
            def body(r, carry):
                row_copy(r).start()
                return carry
            lax.fori_loop(0, n, body, 0)

    def wait_rows(n, all_copy, row_copy):
        @pl.when(n == tb)
        def _():
            all_copy.wait()

        @pl.when(n < tb)
        def _():
            def body(r, carry):
                row_copy.wait()
                return carry
            lax.fori_loop(0, n, body, 0)

    def slots_copy(blk, buf):
        return pltpu.make_async_copy(inv_ref.at[jnp.minimum(blk, n_blocks - 1), 0],
                                     slots.at[pl.ds(buf * tb, tb)], isem.at[buf])

    def nv_of(blk):
        return jnp.where((blk >= 0) & (blk < n_blocks), nv_ref[jnp.clip(blk, 0, n_blocks - 1)], 0)

    nv_next = nv_of(i + 1)

    @pl.when(i == 0)
    def _():
        xbuf[...] = jnp.zeros_like(xbuf)
        slots_copy(0, 0).start()
        slots_copy(0, 0).wait()
        start_rows(nv, lambda r: gather_row(0, r))

        @pl.when(nv_next > 0)
        def _():
            slots_copy(1, 1).start()

    def wait_inputs(buf):
        wait_rows(nv, gather_all(buf), gather_row(buf, 0))
        wait_rows(nv_of(i - 2), scatter_all(buf), scatter_row(buf, 0))

    on_buffer(b, wait_inputs)

    fused_issue = (nv == tb) & (nv_next == tb)

    def prefetch_next(buf):
        slots_copy(i + 1, buf).wait()

        @pl.when(jnp.logical_not(fused_issue))
        def _():
            start_rows(nv_next, lambda r: gather_row(buf, r))

    @pl.when(nv_next > 0)
    def _():
        on_buffer(1 - b, prefetch_next)

    @pl.when((i == 0) | (be_ref[i] != be_ref[jnp.maximum(i - 1, 0)]))
    def _():
        wg_s[...] = wg_ref[0].astype(BF16)
        wu_s[...] = wu_ref[0].astype(BF16)
        wd_s[...] = wd_ref[0].astype(BF16)

    def ffn(issue_next_gathers):
        xb_s[...] = jnp.concatenate([xbuf[b, pl.ds(j, tb, stride=rpt), :] for j in range(rpt)],
                                    axis=1).astype(BF16)
        xb = xb_s[...]
        if issue_next_gathers:
            for r in range(tb):
                gather_row(1 - b, r).start()
        y = None
        for j in range(f // fc):
            cols = slice(j * fc, (j + 1) * fc)
            gate = jnp.minimum(jnp.dot(xb, wg_s[:, cols], preferred_element_type=F32) + bg_ref[0, :, cols],
                               SWIGLU_LIMIT)
            up = jnp.clip(jnp.dot(xb, wu_s[:, cols], preferred_element_type=F32) + bu_ref[0, :, cols],
                          -SWIGLU_LIMIT, SWIGLU_LIMIT)
            act = ((up + 1.0) * (gate * jax.nn.sigmoid(SWIGLU_ALPHA * gate))).astype(BF16)
            part = jnp.dot(act, wd_s[cols, :], preferred_element_type=F32)
            y = part if y is None else y + part
        y = y + bd_ref[0]
        for j in range(rpt):
            ybuf[b, pl.ds(j, tb, stride=rpt), :] = y[:, j * LANES:(j + 1) * LANES]

    @pl.when(fused_issue)
    def _():
        ffn(True)

    @pl.when((nv > 0) & jnp.logical_not(fused_issue))
    def _():
        ffn(False)

    def send_outputs(buf):
        start_rows(nv, lambda r: scatter_row(buf, r))

        @pl.when(nv_of(i + 2) > 0)
        def _():
            slots_copy(i + 2, buf).start()

    on_buffer(b, send_outputs)

    @pl.when(i == n_blocks - 1)
    def _():
        on_buffer(1 - b, lambda buf: wait_rows(nv_of(i - 1), scatter_all(buf), scatter_row(buf, 0)))
        on_buffer(b, lambda buf: wait_rows(nv, scatter_all(buf), scatter_row(buf, 0)))


def _moe(block_e, block_nv, inv, h2t, w_gate, b_gate, w_up, b_up, w_down, b_down, *, tb, fc):
    e, d, f = w_gate.shape
    rpt = d // LANES
    n_tok = h2t.shape[0] // rpt
    n_blocks = block_nv.shape[0]
    assert n_tok & (n_tok - 1) == 0, "token count must be a power of two (slot -> token by masking)"
    assert d == f, "expert width equals model width in this block"
    wspec = lambda shape: pl.BlockSpec((1,) + shape, lambda i, be, nv: (be[i], 0, 0))
    return pl.pallas_call(
        functools.partial(_moe_kernel, tb=tb, fc=fc, n_blocks=n_blocks, n_tok=n_tok),
        grid_spec=pltpu.PrefetchScalarGridSpec(
            num_scalar_prefetch=2,
            grid=(n_blocks,),
            in_specs=[pl.BlockSpec(memory_space=pl.ANY), pl.BlockSpec(memory_space=pl.ANY),
                      wspec((d, f)), wspec((1, f)), wspec((d, f)), wspec((1, f)),
                      wspec((f, d)), wspec((1, d))],
            out_specs=pl.BlockSpec(memory_space=pl.ANY),
            scratch_shapes=[pltpu.VMEM((2, tb * rpt, LANES), F32), pltpu.VMEM((2, tb * rpt, LANES), F32),
                            pltpu.VMEM((tb, d), BF16), pltpu.SMEM((2 * tb,), jnp.int32),
                            pltpu.VMEM((d, f), BF16), pltpu.VMEM((d, f), BF16), pltpu.VMEM((f, d), BF16),
                            pltpu.SemaphoreType.DMA((2,)), pltpu.SemaphoreType.DMA((2,)),
                            pltpu.SemaphoreType.DMA((2,))]),
        out_shape=jax.ShapeDtypeStruct((TOP_K * n_tok * rpt, LANES), F32),
        compiler_params=pltpu.CompilerParams(dimension_semantics=("arbitrary",),
                                             vmem_limit_bytes=VMEM_LIMIT),
        name="moe",
    )(block_e, block_nv, inv, h2t, w_gate, b_gate.reshape(e, 1, f), w_up, b_up.reshape(e, 1, f),
      w_down, b_down.reshape(e, 1, d))


def _combine_kernel(ys_ref, x1_ref, tw_ref, mod_ref, fg_ref, o_ref):
    tw = tw_ref[...]
    tc, d = x1_ref.shape
    rpt = d // LANES

    def expert_rows(k):
        return jnp.concatenate([ys_ref[k, pl.ds(j, tc, stride=rpt), :] for j in range(rpt)], axis=1)

    moe = tw[:, 0:1] * expert_rows(0)
    for k in range(1, TOP_K):
        moe = moe + tw[:, k:k + 1] * expert_rows(k)
    xo = x1_ref[...] + mod_ref[0, 5:6, :] * moe
    o_ref[...] = _rms(xo) * fg_ref[...]


def _combine(ys4, x1, top_w, mod3, final_g, *, tc, seq):
    n_tok, d = x1.shape
    tiles_per_batch = seq // tc
    return pl.pallas_call(
        _combine_kernel,
        grid=(n_tok // tc,),
        in_specs=[pl.BlockSpec((TOP_K, tc * d // LANES, LANES), lambda i: (0, i, 0)),
                  pl.BlockSpec((tc, d), lambda i: (i, 0)),
                  pl.BlockSpec((tc, TOP_K), lambda i: (i, 0)),
                  pl.BlockSpec((1, 6, d), lambda i: (i // tiles_per_batch, 0, 0)),
                  pl.BlockSpec((1, d), lambda i: (0, 0))],
        out_specs=pl.BlockSpec((tc, d), lambda i: (i, 0)),
        out_shape=jax.ShapeDtypeStruct((n_tok, d), F32),
        compiler_params=pltpu.CompilerParams(dimension_semantics=("arbitrary",),
                                             vmem_limit_bytes=VMEM_LIMIT),
        name="combine",
    )(ys4, x1, top_w, mod3, final_g)


def _forward(x, c, ada_w, ada_b, norm1_g, w_in, lam_re, lam_im, log_dt, b_re, b_im, c_re, c_im, d_skip,
             w_glu, b_glu, conv_w, conv_b, ln_g, ln_b, out_norm_g, w_out, norm2_g, router_w, router_b,
             w_gate, b_gate, w_up, b_up, w_down, b_down, final_g, *, tm, rc, tb, fc, tc):
    bsz, seq, d = x.shape
    n_tok = bsz * seq
    cpb = seq // CHUNK
    n_chunks = n_tok // CHUNK
    row = lambda v: v.reshape(1, -1).astype(F32)

    mod3 = _mod(c, ada_w, ada_b).reshape(bsz, 6, d)

    wu_t = w_in[:, :D_SSM].T.astype(BF16)
    wvg = w_in[:, D_SSM:].astype(BF16)
    cw = jnp.zeros((CONV_HALO, D_CONV), F32).at[:CONV_WIDTH].set(conv_w.astype(F32))
    ut, nconv = _inproj(x, mod3, row(norm1_g), wu_t, wvg, cw, row(conv_b), row(ln_g), row(ln_b),
                        row(out_norm_g[D_SSM:]), tm=tm, rc=rc)

    klag, wst, vout, pq = _ssm_params(lam_re, lam_im, log_dt, b_re, b_im, c_re, c_im, cpb)
    dsk = jnp.broadcast_to(d_skip.astype(F32).reshape(N_GROUPS, SSM_GROUP, 1, 1),
                           (N_GROUPS, SSM_GROUP, 1, CHUNK))
    y4 = _ssm(ut.reshape(N_GROUPS, SSM_GROUP, n_chunks, CHUNK), klag, wst, vout, pq, dsk, cpb=cpb)
    yt = y4.reshape(D_SSM, n_chunks, CHUNK)

    rw = router_w.astype(F32)
    rw_hi = rw.astype(BF16)
    rw_split = jnp.concatenate([rw_hi, (rw - rw_hi.astype(F32)).astype(BF16)], axis=1).T
    x1, h2, top_idx, top_w = _mix(
        x, yt, nconv, mod3, w_glu.T.astype(BF16), b_glu.reshape(D_SSM, 1).astype(F32),
        out_norm_g[:D_SSM].reshape(D_SSM, 1).astype(F32), w_out[:D_SSM].astype(BF16),
        w_out[D_SSM:].astype(BF16), row(norm2_g), rw_split, router_b.reshape(N_EXPERTS, 1).astype(F32), tm=tm)

    n_assign = n_tok * TOP_K
    n_blocks = n_assign // tb + N_EXPERTS
    assert n_assign & (n_assign - 1) == 0 and (N_EXPERTS + 1) * n_assign < 2 ** 31
    experts = jnp.arange(N_EXPERTS, dtype=jnp.int32)
    counts = jnp.sum((top_idx[None] == experts[:, None, None]).astype(jnp.int32), axis=(1, 2))
    padded = (counts + tb - 1) // tb * tb
    pend = jnp.cumsum(padded)
    pstart = pend - padded
    blk_start = jnp.arange(n_blocks, dtype=jnp.int32) * tb
    block_e = jnp.minimum(jnp.sum((blk_start[:, None] >= pend[None, :]).astype(jnp.int32), axis=1), N_EXPERTS - 1)
    block_nv = jnp.clip(pstart[block_e] + counts[block_e] - blk_start, 0, tb).astype(jnp.int32)
    block_nv = jnp.where(blk_start < pend[-1], block_nv, 0)

    slot = jnp.arange(n_assign, dtype=jnp.int32).reshape(TOP_K, n_tok)
    pad_end = jnp.cumsum(padded - counts)
    pad_id = jnp.arange(n_blocks * tb - n_assign, dtype=jnp.int32)
    pad_e = jnp.sum((pad_id[:, None] >= pad_end[None, :]).astype(jnp.int32), axis=1)
    keys = jnp.concatenate([(top_idx * n_assign + slot).reshape(-1), pad_e * n_assign + (n_assign - 1)])
    inv = (lax.sort(keys) & (n_assign - 1)).reshape(n_blocks, 1, tb)

    ys4 = _moe(block_e, block_nv, inv, h2, w_gate, b_gate, w_up, b_up, w_down, b_down, tb=tb, fc=fc)
    out = _combine(ys4.reshape(TOP_K, n_tok * d // LANES, LANES), x1.reshape(n_tok, d), top_w, mod3,
                   row(final_g), tc=tc, seq=seq)
    return out.reshape(bsz, seq, d)


def kernel(x, c, ada_w, ada_b, norm1_g, w_in, lam_re, lam_im, log_dt, b_re, b_im, c_re, c_im, d_skip, w_glu, b_glu, conv_w, conv_b, ln_g, ln_b, out_norm_g, w_out, norm2_g, router_w, router_b, w_gate, b_gate, w_up, b_up, w_down, b_down, final_g):
    p = [a[0] for a in (ada_w, ada_b, norm1_g, w_in, lam_re, lam_im, log_dt, b_re, b_im, c_re, c_im, d_skip,
                        w_glu, b_glu, conv_w, conv_b, ln_g, ln_b, out_norm_g, w_out, norm2_g, router_w,
                        router_b, w_gate, b_gate, w_up, b_up, w_down, b_down)]
    return _forward(x, c, *p, final_g, tm=1024, rc=64, tb=512, fc=256, tc=256)
```

```python
import functools
import math

import jax
import jax.numpy as jnp
from jax import lax
from jax.experimental import pallas as pl
from jax.experimental.pallas import tpu as pltpu

F32 = jnp.float32
BF16 = jnp.bfloat16
HIGHEST = lax.Precision.HIGHEST

D_MODEL = 1024
D_SSM = 512
D_CONV = 512
SSM_GROUP = 16
N_GROUPS = 32
SSM_STATE = 64
CONV_WIDTH = 31
N_EXPERTS = 32
TOP_K = 4
SWIGLU_ALPHA = 1.702
SWIGLU_LIMIT = 7.0
RMS_EPS = 1e-6
LN_EPS = 1e-5

CHUNK = 128
CONV_HALO = 32
SUBLANES = 8
LANES = 128
SLOT_RING = 3
VMEM_LIMIT = 56 * 1024 * 1024


def _rms(x, eps=RMS_EPS):
    return x * lax.rsqrt(jnp.mean(x * x, axis=-1, keepdims=True) + eps)


def _mod_kernel(c_ref, w_ref, b_ref, o_ref):
    c = c_ref[...]
    cond = c * jax.nn.sigmoid(c)
    o_ref[...] = jnp.dot(cond, w_ref[...], precision=HIGHEST, preferred_element_type=F32) + b_ref[...]


def _mod(c, ada_w, ada_b):
    bsz, d = c.shape
    n = ada_w.shape[1]
    return pl.pallas_call(
        _mod_kernel,
        grid=(n // d,),
        in_specs=[pl.BlockSpec((bsz, d), lambda j: (0, 0)),
                  pl.BlockSpec((d, d), lambda j: (0, j)),
                  pl.BlockSpec((1, d), lambda j: (0, j))],
        out_specs=pl.BlockSpec((bsz, d), lambda j: (0, j)),
        out_shape=jax.ShapeDtypeStruct((bsz, n), F32),
        name="mod",
    )(c, ada_w, ada_b.reshape(1, n))


def _inproj_kernel(x_ref, mod_ref, g1_ref, wu_ref, wvg_ref, cw_ref, cb_ref, lg_ref, lb_ref, og_ref,
                   ut_ref, nc_ref, zext, zsh, *, tm, rc):
    i = pl.program_id(1)
    x = x_ref[0]
    y = _rms(x) * g1_ref[...]
    h = (y * (1.0 + mod_ref[0, 1:2, :]) + mod_ref[0, 0:1, :]).astype(BF16)
    ut = lax.dot_general(wu_ref[...], h, (((1,), (1,)), ((), ())), preferred_element_type=F32)
    for cc in range(tm // CHUNK):
        ut_ref[:, cc, :] = ut[:, cc * CHUNK:(cc + 1) * CHUNK]
    vg = jnp.dot(h, wvg_ref[...], preferred_element_type=F32)
    z = vg[:, :D_CONV] * jax.nn.sigmoid(vg[:, D_CONV:])

    @pl.when(i == 0)
    def _():
        zext[0:CONV_HALO, :] = jnp.zeros((CONV_HALO, D_CONV), F32)

    zext[CONV_HALO:CONV_HALO + tm, :] = z
    off = CONV_HALO - (CONV_WIDTH - 1)
    span = tm + CONV_HALO - SUBLANES
    for p in range(1, SUBLANES):
        zsh[p - 1, 0:span, :] = zext[p:p + span, :]
    for r0 in range(0, tm, rc):
        acc = jnp.zeros((rc, D_CONV), F32) + cb_ref[...]
        for k in range(CONV_WIDTH):
            p = (off + k) % SUBLANES
            lo = r0 + off + k - p
            tap = zext[lo:lo + rc, :] if p == 0 else zsh[p - 1, lo:lo + rc, :]
            acc = acc + cw_ref[k:k + 1, :] * tap
        mu = jnp.mean(acc, axis=-1, keepdims=True)
        xc = acc - mu
        ln = xc * lax.rsqrt(jnp.mean(xc * xc, axis=-1, keepdims=True) + LN_EPS) * lg_ref[...] + lb_ref[...]
        act = ln * jax.nn.sigmoid(ln)
        nc_ref[0, r0:r0 + rc, :] = (_rms(act) * og_ref[...]).astype(BF16)
    zext[0:CONV_HALO, :] = zext[tm:tm + CONV_HALO, :]


def _inproj(x, mod3, norm1_g, wu_t, wvg, conv_w, conv_b, ln_g, ln_b, og_conv, *, tm, rc):
    bsz, seq, d = x.shape
    nt = seq // tm
    full = lambda shape: pl.BlockSpec(shape, lambda b, i: (0,) * len(shape))
    return pl.pallas_call(
        functools.partial(_inproj_kernel, tm=tm, rc=rc),
        grid=(bsz, nt),
        in_specs=[pl.BlockSpec((1, tm, d), lambda b, i: (b, i, 0)),
                  pl.BlockSpec((1, 6, d), lambda b, i: (b, 0, 0)),
                  full((1, d)), full((D_SSM, d)), full((d, 2 * D_CONV)),
                  full((CONV_HALO, D_CONV)), full((1, D_CONV)), full((1, D_CONV)), full((1, D_CONV)),
                  full((1, D_CONV))],
        out_specs=[pl.BlockSpec((D_SSM, tm // CHUNK, CHUNK), lambda b, i: (0, b * nt + i, 0)),
                   pl.BlockSpec((1, tm, D_CONV), lambda b, i: (b, i, 0))],
        out_shape=[jax.ShapeDtypeStruct((D_SSM, bsz * seq // CHUNK, CHUNK), F32),
                   jax.ShapeDtypeStruct((bsz, seq, D_CONV), BF16)],
        scratch_shapes=[pltpu.VMEM((tm + CONV_HALO, D_CONV), F32),
                        pltpu.VMEM((SUBLANES - 1, tm + CONV_HALO, D_CONV), F32)],
        compiler_params=pltpu.CompilerParams(dimension_semantics=("arbitrary", "arbitrary"),
                                             vmem_limit_bytes=VMEM_LIMIT),
        name="inproj",
    )(x, mod3, norm1_g, wu_t, wvg, conv_w, conv_b, ln_g, ln_b, og_conv)


def _ssm_kernel(u_ref, klag_ref, w_ref, v_ref, pq_ref, dsk_ref, y_ref, toep, *, cpb):
    nc = u_ref.shape[2]
    s_idx = lax.broadcasted_iota(jnp.int32, (CHUNK, CHUNK), 0)
    j_idx = lax.broadcasted_iota(jnp.int32, (CHUNK, CHUNK), 1)
    causal = j_idx >= s_idx

    def expand(hp, carry):
        kv = klag_ref[0, hp]
        blocks = []
        for h in range(SSM_GROUP):
            rows = jnp.broadcast_to(kv[h:h + 1, :], (CHUNK, CHUNK))
            skew = pltpu.roll(rows, 0, 1, stride=1, stride_axis=0)
            blocks.append(jnp.where(causal, skew, 0.0).astype(BF16))
        toep[pl.ds(pl.multiple_of(hp * CHUNK, CHUNK), CHUNK), :] = jnp.concatenate(blocks, axis=1)
        return carry

    lax.fori_loop(0, SSM_GROUP, expand, 0)

    us = [u_ref[0, h] for h in range(SSM_GROUP)]
    xcat = jnp.concatenate([u.astype(BF16) for u in us], axis=1)
    acc = jnp.dot(xcat, toep[...], preferred_element_type=F32)
    st = jnp.dot(xcat, w_ref[0], preferred_element_type=F32)
    row = lax.broadcasted_iota(jnp.int32, (nc, 2 * SSM_STATE), 0) % cpb
    d = 1
    step = 0
    while d < cpb:
        sh = jnp.where(row >= d, pltpu.roll(st, d, axis=0), 0.0)
        st = st + pq_ref[0, step, 0:1, :] * sh + pq_ref[0, step, 1:2, :] * pltpu.roll(sh, SSM_STATE, axis=1)
        d *= 2
        step += 1
    prev = jnp.where(row >= 1, pltpu.roll(st, 1, axis=0), 0.0)
    acc = acc + jnp.dot(prev.astype(BF16), v_ref[0], preferred_element_type=F32)
    for h in range(SSM_GROUP):
        y_ref[0, h] = acc[:, h * CHUNK:(h + 1) * CHUNK] + dsk_ref[0, h] * us[h]


def _ssm(u4, klag, wst, vout, pq, dsk, *, cpb):
    g, hh, nc, t = u4.shape
    blk = lambda shape: pl.BlockSpec((1,) + shape, lambda i: (i,) + (0,) * len(shape))
    return pl.pallas_call(
        functools.partial(_ssm_kernel, cpb=cpb),
        grid=(g,),
        in_specs=[blk((hh, nc, t)), blk(klag.shape[1:]), blk(wst.shape[1:]), blk(vout.shape[1:]),
                  blk(pq.shape[1:]), blk(dsk.shape[1:])],
        out_specs=blk((hh, nc, t)),
        out_shape=jax.ShapeDtypeStruct(u4.shape, F32),
        scratch_shapes=[pltpu.VMEM((hh * t, hh * t), BF16)],
        compiler_params=pltpu.CompilerParams(dimension_semantics=("arbitrary",),
                                             vmem_limit_bytes=VMEM_LIMIT),
        name="ssm",
    )(u4, klag, wst, vout, pq, dsk)


def _ssm_params(lam_re, lam_im, log_dt, b_re, b_im, c_re, c_im, cpb):
    lr, li = lam_re.astype(F32), lam_im.astype(F32)
    dt = jnp.exp(log_dt.astype(F32))[:, None]
    mag = jnp.exp(lr * dt)
    ab_re = mag * jnp.cos(li * dt)
    ab_im = mag * jnp.sin(li * dt)
    den = lr * lr + li * li
    nr = ab_re - 1.0
    q_re = (nr * lr + ab_im * li) / den
    q_im = (ab_im * lr - nr * li) / den
    br, bi = b_re.astype(F32), b_im.astype(F32)
    bb_re = q_re[..., None] * br - q_im[..., None] * bi
    bb_im = q_re[..., None] * bi + q_im[..., None] * br
    cr, ci = c_re.astype(F32), c_im.astype(F32)

    pr, pi = jnp.ones((1,) + ab_re.shape, F32), jnp.zeros((1,) + ab_re.shape, F32)
    cur_r, cur_i = ab_re, ab_im
    while pr.shape[0] < CHUNK:
        pr, pi = (jnp.concatenate([pr, pr * cur_r - pi * cur_i], axis=0),
                  jnp.concatenate([pi, pr * cur_i + pi * cur_r], axis=0))
        cur_r, cur_i = cur_r * cur_r - cur_i * cur_i, 2.0 * cur_r * cur_i
    cp_re = cr[None] * pr[:, :, None, :] - ci[None] * pi[:, :, None, :]
    cp_im = cr[None] * pi[:, :, None, :] + ci[None] * pr[:, :, None, :]
    klag = (jnp.einsum('lghp,gpk->gkhl', cp_re, bb_re, precision=HIGHEST)
            - jnp.einsum('lghp,gpk->gkhl', cp_im, bb_im, precision=HIGHEST))
    rr, ri = pr[::-1], pi[::-1]
    w_re = rr[..., None] * bb_re[None] - ri[..., None] * bb_im[None]
    w_im = rr[..., None] * bb_im[None] + ri[..., None] * bb_re[None]
    wst = jnp.concatenate([w_re, w_im], axis=2).transpose(1, 3, 0, 2)
    wst = wst.reshape(N_GROUPS, SSM_GROUP * CHUNK, 2 * SSM_STATE).astype(BF16)
    p1r, p1i = pr * ab_re - pi * ab_im, pr * ab_im + pi * ab_re
    v_re = cr[None] * p1r[:, :, None, :] - ci[None] * p1i[:, :, None, :]
    v_im = cr[None] * p1i[:, :, None, :] + ci[None] * p1r[:, :, None, :]
    vout = jnp.concatenate([v_re, -v_im], axis=3).transpose(1, 3, 2, 0)
    vout = vout.reshape(N_GROUPS, 2 * SSM_STATE, SSM_GROUP * CHUNK).astype(BF16)
    tabs = []
    d = 1
    while d < cpb or not tabs:
        tabs.append(jnp.stack([jnp.concatenate([cur_r, cur_r], axis=1),
                               jnp.concatenate([-cur_i, cur_i], axis=1)], axis=1))
        cur_r, cur_i = cur_r * cur_r - cur_i * cur_i, 2.0 * cur_r * cur_i
        d *= 2
    pq = jnp.stack(tabs, axis=1)
    return klag, wst, vout, pq


def _mix_kernel(x_ref, yt_ref, nc_ref, mod_ref, wglu_ref, bglu_ref, ogs_ref, wot_ref, wob_ref, g2_ref,
                rw_ref, rb_ref, x1_ref, h2_ref, idx_ref, tw_ref):
    yt = jnp.concatenate([yt_ref[:, cc, :] for cc in range(yt_ref.shape[1])], axis=1)
    yg = 0.5 * yt * (1.0 + jnp.tanh(math.sqrt(2.0 / math.pi) * (yt + 0.044715 * (yt * yt * yt))))
    gate = jnp.dot(wglu_ref[...], yg.astype(BF16), preferred_element_type=F32) + bglu_ref[...]
    y2 = yg * jax.nn.sigmoid(gate)
    ms = jnp.mean(y2 * y2, axis=0, keepdims=True)
    ns = (y2 * lax.rsqrt(ms + RMS_EPS) * ogs_ref[...]).astype(BF16)
    o = (lax.dot_general(ns, wot_ref[...], (((0,), (0,)), ((), ())), preferred_element_type=F32)
         + jnp.dot(nc_ref[0], wob_ref[...], preferred_element_type=F32))
    x1 = x_ref[0] + mod_ref[0, 2:3, :] * o
    x1_ref[0] = x1
    h2 = _rms(x1) * g2_ref[...] * (1.0 + mod_ref[0, 4:5, :]) + mod_ref[0, 3:4, :]
    rpt = h2.shape[1] // LANES
    for j in range(rpt):
        h2_ref[pl.ds(j, h2.shape[0], stride=rpt), :] = h2[:, j * LANES:(j + 1) * LANES]
    nt_dims = (((1,), (1,)), ((), ()))
    h2_hi = h2.astype(BF16)
    h2_lo = (h2 - h2_hi.astype(F32)).astype(BF16)
    l_hi = lax.dot_general(rw_ref[...], h2_hi, nt_dims, preferred_element_type=F32)
    l_lo = lax.dot_general(rw_ref[:N_EXPERTS, :], h2_lo, nt_dims, preferred_element_type=F32)
    logits = l_hi[:N_EXPERTS, :] + (l_hi[N_EXPERTS:, :] + l_lo) + rb_ref[...]
    expert = lax.broadcasted_iota(jnp.int32, logits.shape, 0)
    vals, idxs = [], []
    for _ in range(TOP_K):
        m = jnp.max(logits, axis=0, keepdims=True)
        sel = jnp.min(jnp.where(logits == m, expert, N_EXPERTS), axis=0, keepdims=True)
        vals.append(m)
        idxs.append(sel)
        logits = jnp.where(expert == sel, -jnp.inf, logits)
    es = [jnp.exp(v - vals[0]) for v in vals]
    tot = es[0] + es[1] + es[2] + es[3]
    idx_ref[...] = jnp.concatenate(idxs, axis=0)
    tw_rows = jnp.concatenate([e / tot for e in es] + [jnp.zeros((LANES - TOP_K, h2.shape[0]), F32)], axis=0)
    tw_ref[...] = tw_rows.T[:, :TOP_K]


def _mix(x, yt, nconv, mod3, wglu_t, bglu, og_ssm, wo_top, wo_bot, norm2_g, router_w, router_b, *, tm):
    bsz, seq, d = x.shape
    nt = seq // tm
    n_tok = bsz * seq
    full = lambda shape: pl.BlockSpec(shape, lambda b, i: (0,) * len(shape))
    return pl.pallas_call(
        _mix_kernel,
        grid=(bsz, nt),
        in_specs=[pl.BlockSpec((1, tm, d), lambda b, i: (b, i, 0)),
                  pl.BlockSpec((D_SSM, tm // CHUNK, CHUNK), lambda b, i: (0, b * nt + i, 0)),
                  pl.BlockSpec((1, tm, D_CONV), lambda b, i: (b, i, 0)),
                  pl.BlockSpec((1, 6, d), lambda b, i: (b, 0, 0)),
                  full((D_SSM, D_SSM)), full((D_SSM, 1)), full((D_SSM, 1)),
                  full((D_SSM, d)), full((D_CONV, d)), full((1, d)),
                  full((2 * N_EXPERTS, d)), full((N_EXPERTS, 1))],
        out_specs=[pl.BlockSpec((1, tm, d), lambda b, i: (b, i, 0)),
                   pl.BlockSpec((tm * d // LANES, LANES), lambda b, i: (b * nt + i, 0)),
                   pl.BlockSpec((TOP_K, tm), lambda b, i: (0, b * nt + i)),
                   pl.BlockSpec((tm, TOP_K), lambda b, i: (b * nt + i, 0))],
        out_shape=[jax.ShapeDtypeStruct((bsz, seq, d), F32),
                   jax.ShapeDtypeStruct((n_tok * d // LANES, LANES), F32),
                   jax.ShapeDtypeStruct((TOP_K, n_tok), jnp.int32),
                   jax.ShapeDtypeStruct((n_tok, TOP_K), F32)],
        compiler_params=pltpu.CompilerParams(dimension_semantics=("arbitrary", "arbitrary"),
                                             vmem_limit_bytes=VMEM_LIMIT),
        name="mix",
    )(x, yt, nconv, mod3, wglu_t, bglu, og_ssm, wo_top, wo_bot, norm2_g, router_w, router_b)


def _moe_kernel(be_ref, nv_ref, inv_ref, h_ref, wg_ref, bg_ref, wu_ref, bu_ref, wd_ref, bd_ref,
                ys_ref, xbuf, ybuf, slots, wg_s, wu_s, wd_s, gsem, ssem, isem,
                *, tb, fc, n_blocks, n_tok):
    i = pl.program_id(0)
    b = i % 2
    nv = nv_ref[i]
    f = wg_s.shape[1]
    d = f
    rpt = d // LANES

    def ring(blk):
        return lax.rem(blk, SLOT_RING) * tb

    def gather_row(buf, base, r):
        tok = slots[base + r] & (n_tok - 1)
        return pltpu.make_async_copy(h_ref.at[pl.ds(pl.multiple_of(tok * rpt, rpt), rpt), :],
                                     xbuf.at[buf, pl.ds(pl.multiple_of(r * rpt, rpt), rpt), :], gsem.at[buf])

    def scatter_row(buf, base, r):
        return pltpu.make_async_copy(ybuf.at[buf, pl.ds(pl.multiple_of(r * rpt, rpt), rpt), :],
                                     ys_ref.at[pl.ds(pl.multiple_of(slots[base + r] * rpt, rpt), rpt), :],
                                     ssem.at[buf])

    def gather_all(buf):
        return pltpu.make_async_copy(h_ref.at[pl.ds(0, tb * rpt), :], xbuf.at[buf], gsem.at[buf])

    def scatter_all(buf):
        return pltpu.make_async_copy(ybuf.at[buf], ys_ref.at[pl.ds(0, tb * rpt), :], ssem.at[buf])

    def gather_one(buf):
        return pltpu.make_async_copy(h_ref.at[pl.ds(0, rpt), :], xbuf.at[buf, pl.ds(0, rpt), :], gsem.at[buf])

    def scatter_one(buf):
        return pltpu.make_async_copy(ybuf.at[buf, pl.ds(0, rpt), :], ys_ref.at[pl.ds(0, rpt), :], ssem.at[buf])

    def on_buffer(which, fn):
        for buf in (0, 1):
            @pl.when(which == buf)
            def _():
                fn(buf)

    def start_rows(n, row_copy):
        @pl.when(n == tb)
        def _():
            for r in range(tb):
                row_copy(r).start()

        @pl.when(n < tb)
        def _():
            def body(r, carry):
                row_copy(r).start()
                return carry
            lax.fori_loop(0, n, body, 0)

    def wait_rows(n, all_copy, row_copy):
        @pl.when(n == tb)
        def _():
            all_copy.wait()

        @pl.when(n < tb)
        def _():
            def body(r, carry):
                row_copy.wait()
                return carry
            lax.fori_loop(0, n, body, 0)

    def slots_copy(blk):
        which = lax.rem(blk, SLOT_RING)
        return pltpu.make_async_copy(inv_ref.at[jnp.minimum(blk, n_blocks - 1), 0],
                                     slots.at[pl.ds(which * tb, tb)], isem.at[which])

    def nv_of(blk):
        return jnp.where((blk >= 0) & (blk < n_blocks), nv_ref[jnp.clip(blk, 0, n_blocks - 1)], 0)

    nv_next = nv_of(i + 1)

    @pl.when(i == 0)
    def _():
        xbuf[...] = jnp.zeros_like(xbuf)
        slots_copy(0).start()
        slots_copy(0).wait()
        start_rows(nv, lambda r: gather_row(0, 0, r))

        @pl.when(nv_next > 0)
        def _():
            slots_copy(1).start()

    def wait_inputs(buf):
        wait_rows(nv, gather_all(buf), gather_one(buf))
        wait_rows(nv_of(i - 2), scatter_all(buf), scatter_one(buf))

    on_buffer(b, wait_inputs)

    @pl.when(nv_of(i + 2) > 0)
    def _():
        slots_copy(i + 2).start()

    fused_issue = (nv == tb) & (nv_next == tb)

    def prefetch_next(buf):
        slots_copy(i + 1).wait()
        base = ring(i + 1)

        @pl.when(jnp.logical_not(fused_issue))
        def _():
            start_rows(nv_next, lambda r: gather_row(buf, base, r))

    @pl.when(nv_next > 0)
    def _():
        on_buffer(1 - b, prefetch_next)

    @pl.when((i == 0) | (be_ref[i] != be_ref[jnp.maximum(i - 1, 0)]))
    def _():
        wg_s[...] = wg_ref[0].astype(BF16)
        wu_s[...] = wu_ref[0].astype(BF16)
        wd_s[...] = wd_ref[0].astype(BF16)

    def ffn(issue_next_gathers):
        xb = jnp.concatenate([xbuf[b, pl.ds(j, tb, stride=rpt), :] for j in range(rpt)], axis=1).astype(BF16)
        y = None
        n_chunks = f // fc
        base_next = ring(i + 1)
        for j in range(n_chunks):
            cols = slice(j * fc, (j + 1) * fc)
            gate = jnp.minimum(jnp.dot(xb, wg_s[:, cols], preferred_element_type=F32) + bg_ref[0, :, cols],
                               SWIGLU_LIMIT)
            up = jnp.clip(jnp.dot(xb, wu_s[:, cols], preferred_element_type=F32) + bu_ref[0, :, cols],
                          -SWIGLU_LIMIT, SWIGLU_LIMIT)
            if issue_next_gathers:
                for r in range(j * tb // n_chunks, (j + 1) * tb // n_chunks):
                    gather_row(1 - b, base_next, r).start()
            act = ((up + 1.0) * (gate * jax.nn.sigmoid(SWIGLU_ALPHA * gate))).astype(BF16)
            part = jnp.dot(act, wd_s[cols, :], preferred_element_type=F32)
            y = part if y is None else y + part
        y = y + bd_ref[0]
        for j in range(rpt):
            ybuf[b, pl.ds(j, tb, stride=rpt), :] = y[:, j * LANES:(j + 1) * LANES]

    @pl.when(fused_issue)
    def _():
        ffn(True)

    @pl.when((nv > 0) & jnp.logical_not(fused_issue))
    def _():
        ffn(False)

    def send_outputs(buf):
        base = ring(i)
        start_rows(nv, lambda r: scatter_row(buf, base, r))

    on_buffer(b, send_outputs)

    @pl.when(i == n_blocks - 1)
    def _():
        on_buffer(1 - b, lambda buf: wait_rows(nv_of(i - 1), scatter_all(buf), scatter_one(buf)))
        on_buffer(b, lambda buf: wait_rows(nv, scatter_all(buf), scatter_one(buf)))


def _moe(block_e, block_nv, inv, h2t, w_gate, b_gate, w_up, b_up, w_down, b_down, *, tb, fc):
    e, d, f = w_gate.shape
    rpt = d // LANES
    n_tok = h2t.shape[0] // rpt
    n_blocks = block_nv.shape[0]
    assert n_tok & (n_tok - 1) == 0, "token count must be a power of two (slot -> token by masking)"
    assert d == f, "expert width equals model width in this block"
    wspec = lambda shape: pl.BlockSpec((1,) + shape, lambda i, be, nv: (be[i], 0, 0))
    return pl.pallas_call(
        functools.partial(_moe_kernel, tb=tb, fc=fc, n_blocks=n_blocks, n_tok=n_tok),
        grid_spec=pltpu.PrefetchScalarGridSpec(
            num_scalar_prefetch=2,
            grid=(n_blocks,),
            in_specs=[pl.BlockSpec(memory_space=pl.ANY), pl.BlockSpec(memory_space=pl.ANY),
                      wspec((d, f)), wspec((1, f)), wspec((d, f)), wspec((1, f)),
                      wspec((f, d)), wspec((1, d))],
            out_specs=pl.BlockSpec(memory_space=pl.ANY),
            scratch_shapes=[pltpu.VMEM((2, tb * rpt, LANES), F32), pltpu.VMEM((2, tb * rpt, LANES), F32),
                            pltpu.SMEM((SLOT_RING * tb,), jnp.int32),
                            pltpu.VMEM((d, f), BF16), pltpu.VMEM((d, f), BF16), pltpu.VMEM((f, d), BF16),
                            pltpu.SemaphoreType.DMA((2,)), pltpu.SemaphoreType.DMA((2,)),
                            pltpu.SemaphoreType.DMA((SLOT_RING,))]),
        out_shape=jax.ShapeDtypeStruct((TOP_K * n_tok * rpt, LANES), F32),
        compiler_params=pltpu.CompilerParams(dimension_semantics=("arbitrary",),
                                             vmem_limit_bytes=VMEM_LIMIT),
        name="moe",
    )(block_e, block_nv, inv, h2t, w_gate, b_gate.reshape(e, 1, f), w_up, b_up.reshape(e, 1, f),
      w_down, b_down.reshape(e, 1, d))


def _combine_kernel(ys_ref, x1_ref, tw_ref, mod_ref, fg_ref, o_ref):
    tw = tw_ref[...]
    tc, d = x1_ref.shape
    rpt = d // LANES

    def expert_rows(k):
        return jnp.concatenate([ys_ref[k, pl.ds(j, tc, stride=rpt), :] for j in range(rpt)], axis=1)

    moe = tw[:, 0:1] * expert_rows(0)
    for k in range(1, TOP_K):
        moe = moe + tw[:, k:k + 1] * expert_rows(k)
    xo = x1_ref[...] + mod_ref[0, 5:6, :] * moe
    o_ref[...] = _rms(xo) * fg_ref[...]


def _combine(ys4, x1, top_w, mod3, final_g, *, tc, seq):
    n_tok, d = x1.shape
    tiles_per_batch = seq // tc
    return pl.pallas_call(
        _combine_kernel,
        grid=(n_tok // tc,),
        in_specs=[pl.BlockSpec((TOP_K, tc * d // LANES, LANES), lambda i: (0, i, 0)),
                  pl.BlockSpec((tc, d), lambda i: (i, 0)),
                  pl.BlockSpec((tc, TOP_K), lambda i: (i, 0)),
                  pl.BlockSpec((1, 6, d), lambda i: (i // tiles_per_batch, 0, 0)),
                  pl.BlockSpec((1, d), lambda i: (0, 0))],
        out_specs=pl.BlockSpec((tc, d), lambda i: (i, 0)),
        out_shape=jax.ShapeDtypeStruct((n_tok, d), F32),
        compiler_params=pltpu.CompilerParams(dimension_semantics=("arbitrary",),
                                             vmem_limit_bytes=VMEM_LIMIT),
        name="combine",
    )(ys4, x1, top_w, mod3, final_g)


def _forward(x, c, ada_w, ada_b, norm1_g, w_in, lam_re, lam_im, log_dt, b_re, b_im, c_re, c_im, d_skip,
             w_glu, b_glu, conv_w, conv_b, ln_g, ln_b, out_norm_g, w_out, norm2_g, router_w, router_b,
             w_gate, b_gate, w_up, b_up, w_down, b_down, final_g, *, tm, rc, tb, fc, tc):
    bsz, seq, d = x.shape
    n_tok = bsz * seq
    cpb = seq // CHUNK
    n_chunks = n_tok // CHUNK
    row = lambda v: v.reshape(1, -1).astype(F32)

    mod3 = _mod(c, ada_w, ada_b).reshape(bsz, 6, d)

    wu_t = w_in[:, :D_SSM].T.astype(BF16)
    wvg = w_in[:, D_SSM:].astype(BF16)
    cw = jnp.zeros((CONV_HALO, D_CONV), F32).at[:CONV_WIDTH].set(conv_w.astype(F32))
    ut, nconv = _inproj(x, mod3, row(norm1_g), wu_t, wvg, cw, row(conv_b), row(ln_g), row(ln_b),
                        row(out_norm_g[D_SSM:]), tm=tm, rc=rc)

    klag, wst, vout, pq = _ssm_params(lam_re, lam_im, log_dt, b_re, b_im, c_re, c_im, cpb)
    dsk = jnp.broadcast_to(d_skip.astype(F32).reshape(N_GROUPS, SSM_GROUP, 1, 1),
                           (N_GROUPS, SSM_GROUP, 1, CHUNK))
    y4 = _ssm(ut.reshape(N_GROUPS, SSM_GROUP, n_chunks, CHUNK), klag, wst, vout, pq, dsk, cpb=cpb)
    yt = y4.reshape(D_SSM, n_chunks, CHUNK)

    rw = router_w.astype(F32)
    rw_hi = rw.astype(BF16)
    rw_split = jnp.concatenate([rw_hi, (rw - rw_hi.astype(F32)).astype(BF16)], axis=1).T
    x1, h2, top_idx, top_w = _mix(
        x, yt, nconv, mod3, w_glu.T.astype(BF16), b_glu.reshape(D_SSM, 1).astype(F32),
        out_norm_g[:D_SSM].reshape(D_SSM, 1).astype(F32), w_out[:D_SSM].astype(BF16),
        w_out[D_SSM:].astype(BF16), row(norm2_g), rw_split, router_b.reshape(N_EXPERTS, 1).astype(F32), tm=tm)

    n_assign = n_tok * TOP_K
    n_blocks = n_assign // tb + N_EXPERTS
    assert n_assign & (n_assign - 1) == 0 and (N_EXPERTS + 1) * n_assign < 2 ** 31
    experts = jnp.arange(N_EXPERTS, dtype=jnp.int32)
    counts = jnp.sum((top_idx[None] == experts[:, None, None]).astype(jnp.int32), axis=(1, 2))
    padded = (counts + tb - 1) // tb * tb
    pend = jnp.cumsum(padded)
    pstart = pend - padded
    blk_start = jnp.arange(n_blocks, dtype=jnp.int32) * tb
    block_e = jnp.minimum(jnp.sum((blk_start[:, None] >= pend[None, :]).astype(jnp.int32), axis=1), N_EXPERTS - 1)
    block_nv = jnp.clip(pstart[block_e] + counts[block_e] - blk_start, 0, tb).astype(jnp.int32)
    block_nv = jnp.where(blk_start < pend[-1], block_nv, 0)

    slot = jnp.arange(n_assign, dtype=jnp.int32).reshape(TOP_K, n_tok)
    pad_end = jnp.cumsum(padded - counts)
    pad_id = jnp.arange(n_blocks * tb - n_assign, dtype=jnp.int32)
    pad_e = jnp.sum((pad_id[:, None] >= pad_end[None, :]).astype(jnp.int32), axis=1)
    keys = jnp.concatenate([(top_idx * n_assign + slot).reshape(-1), pad_e * n_assign + (n_assign - 1)])
    inv = (lax.sort(keys) & (n_assign - 1)).reshape(n_blocks, 1, tb)

    ys4 = _moe(block_e, block_nv, inv, h2, w_gate, b_gate, w_up, b_up, w_down, b_down, tb=tb, fc=fc)
    out = _combine(ys4.reshape(TOP_K, n_tok * d // LANES, LANES), x1.reshape(n_tok, d), top_w, mod3,
                   row(final_g), tc=tc, seq=seq)
    return out.reshape(bsz, seq, d)


def kernel(x, c, ada_w, ada_b, norm1_g, w_in, lam_re, lam_im, log_dt, b_re, b_im, c_re, c_im, d_skip, w_glu, b_glu, conv_w, conv_b, ln_g, ln_b, out_norm_g, w_out, norm2_g, router_w, router_b, w_gate, b_gate, w_up, b_up, w_down, b_down, final_g):
    p = [a[0] for a in (ada_w, ada_b, norm1_g, w_in, lam_re, lam_im, log_dt, b_re, b_im, c_re, c_im, d_skip,
                        w_glu, b_glu, conv_w, conv_b, ln_g, ln_b, out_norm_g, w_out, norm2_g, router_w,
                        router_b, w_gate, b_gate, w_up, b_up, w_down, b_down)]
    return _forward(x, c, *p, final_g, tm=1024, rc=64, tb=512, fc=256, tc=256)
```

```python
import functools
import math

import jax
import jax.numpy as jnp
from jax import lax
from jax.experimental import pallas as pl
from jax.experimental.pallas import tpu as pltpu

F32 = jnp.float32
BF16 = jnp.bfloat16
HIGHEST = lax.Precision.HIGHEST

D_MODEL = 1024
D_SSM = 512
D_CONV = 512
SSM_GROUP = 16
N_GROUPS = 32
SSM_STATE = 64
CONV_WIDTH = 31
N_EXPERTS = 32
TOP_K = 4
SWIGLU_ALPHA = 1.702
SWIGLU_LIMIT = 7.0
RMS_EPS = 1e-6
LN_EPS = 1e-5

CHUNK = 128
CONV_HALO = 32
SUBLANES = 8
LANES = 128
SCATTER_DMA_PRIORITY = 1
SLOT_RING = 3
VMEM_LIMIT = 56 * 1024 * 1024


def _rms(x, eps=RMS_EPS):
    return x * lax.rsqrt(jnp.mean(x * x, axis=-1, keepdims=True) + eps)


def _mod_kernel(c_ref, w_ref, b_ref, o_ref):
    c = c_ref[...]
    cond = c * jax.nn.sigmoid(c)
    o_ref[...] = jnp.dot(cond, w_ref[...], precision=HIGHEST, preferred_element_type=F32) + b_ref[...]


def _mod(c, ada_w, ada_b):
    bsz, d = c.shape
    n = ada_w.shape[1]
    return pl.pallas_call(
        _mod_kernel,
        grid=(n // d,),
        in_specs=[pl.BlockSpec((bsz, d), lambda j: (0, 0)),
                  pl.BlockSpec((d, d), lambda j: (0, j)),
                  pl.BlockSpec((1, d), lambda j: (0, j))],
        out_specs=pl.BlockSpec((bsz, d), lambda j: (0, j)),
        out_shape=jax.ShapeDtypeStruct((bsz, n), F32),
        name="mod",
    )(c, ada_w, ada_b.reshape(1, n))


def _inproj_kernel(x_ref, mod_ref, g1_ref, wu_ref, wvg_ref, cw_ref, cb_ref, lg_ref, lb_ref, og_ref,
                   ut_ref, nc_ref, zext, zsh, *, tm, rc):
    i = pl.program_id(1)
    x = x_ref[0]
    y = _rms(x) * g1_ref[...]
    h = (y * (1.0 + mod_ref[0, 1:2, :]) + mod_ref[0, 0:1, :]).astype(BF16)
    ut = lax.dot_general(wu_ref[...], h, (((1,), (1,)), ((), ())), preferred_element_type=F32)
    for cc in range(tm // CHUNK):
        ut_ref[:, cc, :] = ut[:, cc * CHUNK:(cc + 1) * CHUNK]
    vg = jnp.dot(h, wvg_ref[...], preferred_element_type=F32)
    z = vg[:, :D_CONV] * jax.nn.sigmoid(vg[:, D_CONV:])

    @pl.when(i == 0)
    def _():
        zext[0:CONV_HALO, :] = jnp.zeros((CONV_HALO, D_CONV), F32)

    zext[CONV_HALO:CONV_HALO + tm, :] = z
    off = CONV_HALO - (CONV_WIDTH - 1)
    span = tm + CONV_HALO - SUBLANES
    for p in range(1, SUBLANES):
        zsh[p - 1, 0:span, :] = zext[p:p + span, :]
    for r0 in range(0, tm, rc):
        acc = jnp.zeros((rc, D_CONV), F32) + cb_ref[...]
        for k in range(CONV_WIDTH):
            p = (off + k) % SUBLANES
            lo = r0 + off + k - p
            tap = zext[lo:lo + rc, :] if p == 0 else zsh[p - 1, lo:lo + rc, :]
            acc = acc + cw_ref[k:k + 1, :] * tap
        mu = jnp.mean(acc, axis=-1, keepdims=True)
        xc = acc - mu
        ln = xc * lax.rsqrt(jnp.mean(xc * xc, axis=-1, keepdims=True) + LN_EPS) * lg_ref[...] + lb_ref[...]
        act = ln * jax.nn.sigmoid(ln)
        nc_ref[0, r0:r0 + rc, :] = (_rms(act) * og_ref[...]).astype(BF16)
    zext[0:CONV_HALO, :] = zext[tm:tm + CONV_HALO, :]


def _inproj(x, mod3, norm1_g, wu_t, wvg, conv_w, conv_b, ln_g, ln_b, og_conv, *, tm, rc):
    bsz, seq, d = x.shape
    nt = seq // tm
    full = lambda shape: pl.BlockSpec(shape, lambda b, i: (0,) * len(shape))
    return pl.pallas_call(
        functools.partial(_inproj_kernel, tm=tm, rc=rc),
        grid=(bsz, nt),
        in_specs=[pl.BlockSpec((1, tm, d), lambda b, i: (b, i, 0)),
                  pl.BlockSpec((1, 6, d), lambda b, i: (b, 0, 0)),
                  full((1, d)), full((D_SSM, d)), full((d, 2 * D_CONV)),
                  full((CONV_HALO, D_CONV)), full((1, D_CONV)), full((1, D_CONV)), full((1, D_CONV)),
                  full((1, D_CONV))],
        out_specs=[pl.BlockSpec((D_SSM, tm // CHUNK, CHUNK), lambda b, i: (0, b * nt + i, 0)),
                   pl.BlockSpec((1, tm, D_CONV), lambda b, i: (b, i, 0))],
        out_shape=[jax.ShapeDtypeStruct((D_SSM, bsz * seq // CHUNK, CHUNK), F32),
                   jax.ShapeDtypeStruct((bsz, seq, D_CONV), BF16)],
        scratch_shapes=[pltpu.VMEM((tm + CONV_HALO, D_CONV), F32),
                        pltpu.VMEM((SUBLANES - 1, tm + CONV_HALO, D_CONV), F32)],
        compiler_params=pltpu.CompilerParams(dimension_semantics=("arbitrary", "arbitrary"),
                                             vmem_limit_bytes=VMEM_LIMIT),
        name="inproj",
    )(x, mod3, norm1_g, wu_t, wvg, conv_w, conv_b, ln_g, ln_b, og_conv)


def _ssm_kernel(u_ref, klag_ref, w_ref, v_ref, pq_ref, dsk_ref, y_ref, toep, *, cpb):
    nc = u_ref.shape[2]
    s_idx = lax.broadcasted_iota(jnp.int32, (CHUNK, CHUNK), 0)
    j_idx = lax.broadcasted_iota(jnp.int32, (CHUNK, CHUNK), 1)
    causal = j_idx >= s_idx

    def expand(hp, carry):
        kv = klag_ref[0, hp]
        blocks = []
        for h in range(SSM_GROUP):
            rows = jnp.broadcast_to(kv[h:h + 1, :], (CHUNK, CHUNK))
            skew = pltpu.roll(rows, 0, 1, stride=1, stride_axis=0)
            blocks.append(jnp.where(causal, skew, 0.0).astype(BF16))
        toep[pl.ds(pl.multiple_of(hp * CHUNK, CHUNK), CHUNK), :] = jnp.concatenate(blocks, axis=1)
        return carry

    lax.fori_loop(0, SSM_GROUP, expand, 0)

    us = [u_ref[0, h] for h in range(SSM_GROUP)]
    xcat = jnp.concatenate([u.astype(BF16) for u in us], axis=1)
    acc = jnp.dot(xcat, toep[...], preferred_element_type=F32)
    st = jnp.dot(xcat, w_ref[0], preferred_element_type=F32)
    row = lax.broadcasted_iota(jnp.int32, (nc, 2 * SSM_STATE), 0) % cpb
    d = 1
    step = 0
    while d < cpb:
        sh = jnp.where(row >= d, pltpu.roll(st, d, axis=0), 0.0)
        st = st + pq_ref[0, step, 0:1, :] * sh + pq_ref[0, step, 1:2, :] * pltpu.roll(sh, SSM_STATE, axis=1)
        d *= 2
        step += 1
    prev = jnp.where(row >= 1, pltpu.roll(st, 1, axis=0), 0.0)
    acc = acc + jnp.dot(prev.astype(BF16), v_ref[0], preferred_element_type=F32)
    for h in range(SSM_GROUP):
        y_ref[0, h] = acc[:, h * CHUNK:(h + 1) * CHUNK] + dsk_ref[0, h] * us[h]


def _ssm(u4, klag, wst, vout, pq, dsk, *, cpb):
    g, hh, nc, t = u4.shape
    blk = lambda shape: pl.BlockSpec((1,) + shape, lambda i: (i,) + (0,) * len(shape))
    return pl.pallas_call(
        functools.partial(_ssm_kernel, cpb=cpb),
        grid=(g,),
        in_specs=[blk((hh, nc, t)), blk(klag.shape[1:]), blk(wst.shape[1:]), blk(vout.shape[1:]),
                  blk(pq.shape[1:]), blk(dsk.shape[1:])],
        out_specs=blk((hh, nc, t)),
        out_shape=jax.ShapeDtypeStruct(u4.shape, F32),
        scratch_shapes=[pltpu.VMEM((hh * t, hh * t), BF16)],
        compiler_params=pltpu.CompilerParams(dimension_semantics=("arbitrary",),
                                             vmem_limit_bytes=VMEM_LIMIT),
        name="ssm",
    )(u4, klag, wst, vout, pq, dsk)


def _ssm_params(lam_re, lam_im, log_dt, b_re, b_im, c_re, c_im, cpb):
    lr, li = lam_re.astype(F32), lam_im.astype(F32)
    dt = jnp.exp(log_dt.astype(F32))[:, None]
    mag = jnp.exp(lr * dt)
    ab_re = mag * jnp.cos(li * dt)
    ab_im = mag * jnp.sin(li * dt)
    den = lr * lr + li * li
    nr = ab_re - 1.0
    q_re = (nr * lr + ab_im * li) / den
    q_im = (ab_im * lr - nr * li) / den
    br, bi = b_re.astype(F32), b_im.astype(F32)
    bb_re = q_re[..., None] * br - q_im[..., None] * bi
    bb_im = q_re[..., None] * bi + q_im[..., None] * br
    cr, ci = c_re.astype(F32), c_im.astype(F32)

    pr, pi = jnp.ones((1,) + ab_re.shape, F32), jnp.zeros((1,) + ab_re.shape, F32)
    cur_r, cur_i = ab_re, ab_im
    while pr.shape[0] < CHUNK:
        pr, pi = (jnp.concatenate([pr, pr * cur_r - pi * cur_i], axis=0),
                  jnp.concatenate([pi, pr * cur_i + pi * cur_r], axis=0))
        cur_r, cur_i = cur_r * cur_r - cur_i * cur_i, 2.0 * cur_r * cur_i
    cp_re = cr[None] * pr[:, :, None, :] - ci[None] * pi[:, :, None, :]
    cp_im = cr[None] * pi[:, :, None, :] + ci[None] * pr[:, :, None, :]
    klag = (jnp.einsum('lghp,gpk->gkhl', cp_re, bb_re, precision=HIGHEST)
            - jnp.einsum('lghp,gpk->gkhl', cp_im, bb_im, precision=HIGHEST))
    rr, ri = pr[::-1], pi[::-1]
    w_re = rr[..., None] * bb_re[None] - ri[..., None] * bb_im[None]
    w_im = rr[..., None] * bb_im[None] + ri[..., None] * bb_re[None]
    wst = jnp.concatenate([w_re, w_im], axis=2).transpose(1, 3, 0, 2)
    wst = wst.reshape(N_GROUPS, SSM_GROUP * CHUNK, 2 * SSM_STATE).astype(BF16)
    p1r, p1i = pr * ab_re - pi * ab_im, pr * ab_im + pi * ab_re
    v_re = cr[None] * p1r[:, :, None, :] - ci[None] * p1i[:, :, None, :]
    v_im = cr[None] * p1i[:, :, None, :] + ci[None] * p1r[:, :, None, :]
    vout = jnp.concatenate([v_re, -v_im], axis=3).transpose(1, 3, 2, 0)
    vout = vout.reshape(N_GROUPS, 2 * SSM_STATE, SSM_GROUP * CHUNK).astype(BF16)
    tabs = []
    d = 1
    while d < cpb or not tabs:
        tabs.append(jnp.stack([jnp.concatenate([cur_r, cur_r], axis=1),
                               jnp.concatenate([-cur_i, cur_i], axis=1)], axis=1))
        cur_r, cur_i = cur_r * cur_r - cur_i * cur_i, 2.0 * cur_r * cur_i
        d *= 2
    pq = jnp.stack(tabs, axis=1)
    return klag, wst, vout, pq


def _mix_kernel(x_ref, yt_ref, nc_ref, mod_ref, wglu_ref, bglu_ref, ogs_ref, wot_ref, wob_ref, g2_ref,
                rw_ref, rb_ref, x1_ref, h2_ref, idx_ref, tw_ref):
    yt = jnp.concatenate([yt_ref[:, cc, :] for cc in range(yt_ref.shape[1])], axis=1)
    yg = 0.5 * yt * (1.0 + jnp.tanh(math.sqrt(2.0 / math.pi) * (yt + 0.044715 * (yt * yt * yt))))
    gate = jnp.dot(wglu_ref[...], yg.astype(BF16), preferred_element_type=F32) + bglu_ref[...]
    y2 = yg * jax.nn.sigmoid(gate)
    ms = jnp.mean(y2 * y2, axis=0, keepdims=True)
    ns = (y2 * lax.rsqrt(ms + RMS_EPS) * ogs_ref[...]).astype(BF16)
    o = (lax.dot_general(ns, wot_ref[...], (((0,), (0,)), ((), ())), preferred_element_type=F32)
         + jnp.dot(nc_ref[0], wob_ref[...], preferred_element_type=F32))
    x1 = x_ref[0] + mod_ref[0, 2:3, :] * o
    x1_ref[0] = x1
    h2 = _rms(x1) * g2_ref[...] * (1.0 + mod_ref[0, 4:5, :]) + mod_ref[0, 3:4, :]
    rpt = h2.shape[1] // LANES
    for j in range(rpt):
        h2_ref[pl.ds(j, h2.shape[0], stride=rpt), :] = h2[:, j * LANES:(j + 1) * LANES]
    nt_dims = (((1,), (1,)), ((), ()))
    h2_hi = h2.astype(BF16)
    h2_lo = (h2 - h2_hi.astype(F32)).astype(BF16)
    l_hi = lax.dot_general(rw_ref[...], h2_hi, nt_dims, preferred_element_type=F32)
    l_lo = lax.dot_general(rw_ref[:N_EXPERTS, :], h2_lo, nt_dims, preferred_element_type=F32)
    logits = l_hi[:N_EXPERTS, :] + (l_hi[N_EXPERTS:, :] + l_lo) + rb_ref[...]
    expert = lax.broadcasted_iota(jnp.int32, logits.shape, 0)
    vals, idxs = [], []
    for _ in range(TOP_K):
        m = jnp.max(logits, axis=0, keepdims=True)
        sel = jnp.min(jnp.where(logits == m, expert, N_EXPERTS), axis=0, keepdims=True)
        vals.append(m)
        idxs.append(sel)
        logits = jnp.where(expert == sel, -jnp.inf, logits)
    es = [jnp.exp(v - vals[0]) for v in vals]
    tot = es[0] + es[1] + es[2] + es[3]
    idx_ref[...] = jnp.concatenate(idxs, axis=0)
    tw_rows = jnp.concatenate([e / tot for e in es] + [jnp.zeros((LANES - TOP_K, h2.shape[0]), F32)], axis=0)
    tw_ref[...] = tw_rows.T[:, :TOP_K]


def _mix(x, yt, nconv, mod3, wglu_t, bglu, og_ssm, wo_top, wo_bot, norm2_g, router_w, router_b, *, tm):
    bsz, seq, d = x.shape
    nt = seq // tm
    n_tok = bsz * seq
    full = lambda shape: pl.BlockSpec(shape, lambda b, i: (0,) * len(shape))
    return pl.pallas_call(
        _mix_kernel,
        grid=(bsz, nt),
        in_specs=[pl.BlockSpec((1, tm, d), lambda b, i: (b, i, 0)),
                  pl.BlockSpec((D_SSM, tm // CHUNK, CHUNK), lambda b, i: (0, b * nt + i, 0)),
                  pl.BlockSpec((1, tm, D_CONV), lambda b, i: (b, i, 0)),
                  pl.BlockSpec((1, 6, d), lambda b, i: (b, 0, 0)),
                  full((D_SSM, D_SSM)), full((D_SSM, 1)), full((D_SSM, 1)),
                  full((D_SSM, d)), full((D_CONV, d)), full((1, d)),
                  full((2 * N_EXPERTS, d)), full((N_EXPERTS, 1))],
        out_specs=[pl.BlockSpec((1, tm, d), lambda b, i: (b, i, 0)),
                   pl.BlockSpec((tm * d // LANES, LANES), lambda b, i: (b * nt + i, 0)),
                   pl.BlockSpec((TOP_K, tm), lambda b, i: (0, b * nt + i)),
                   pl.BlockSpec((tm, TOP_K), lambda b, i: (b * nt + i, 0))],
        out_shape=[jax.ShapeDtypeStruct((bsz, seq, d), F32),
                   jax.ShapeDtypeStruct((n_tok * d // LANES, LANES), F32),
                   jax.ShapeDtypeStruct((TOP_K, n_tok), jnp.int32),
                   jax.ShapeDtypeStruct((n_tok, TOP_K), F32)],
        compiler_params=pltpu.CompilerParams(dimension_semantics=("arbitrary", "arbitrary"),
                                             vmem_limit_bytes=VMEM_LIMIT),
        name="mix",
    )(x, yt, nconv, mod3, wglu_t, bglu, og_ssm, wo_top, wo_bot, norm2_g, router_w, router_b)


def _moe_kernel(be_ref, nv_ref, inv_ref, h_ref, wg_ref, bg_ref, wu_ref, bu_ref, wd_ref, bd_ref,
                ys_ref, xbuf, ybuf, slots, wg_s, wu_s, wd_s, gsem, ssem, isem,
                *, tb, fc, n_blocks, n_tok):
    i = pl.program_id(0)
    b = i % 2
    nv = nv_ref[i]
    f = wg_s.shape[1]
    d = f
    rpt = d // LANES

    def ring(blk):
        return lax.rem(blk, SLOT_RING) * tb

    def gather_row(buf, base, r):
        tok = slots[base + r] & (n_tok - 1)
        return pltpu.make_async_copy(h_ref.at[pl.ds(pl.multiple_of(tok * rpt, rpt), rpt), :],
                                     xbuf.at[buf, pl.ds(pl.multiple_of(r * rpt, rpt), rpt), :], gsem.at[buf])

    def scatter_row(buf, base, r):
        return pltpu.make_async_copy(ybuf.at[buf, pl.ds(pl.multiple_of(r * rpt, rpt), rpt), :],
                                     ys_ref.at[pl.ds(pl.multiple_of(slots[base + r] * rpt, rpt), rpt), :],
                                     ssem.at[buf])

    def gather_all(buf):
        return pltpu.make_async_copy(h_ref.at[pl.ds(0, tb * rpt), :], xbuf.at[buf], gsem.at[buf])

    def scatter_all(buf):
        return pltpu.make_async_copy(ybuf.at[buf], ys_ref.at[pl.ds(0, tb * rpt), :], ssem.at[buf])

    def gather_one(buf):
        return pltpu.make_async_copy(h_ref.at[pl.ds(0, rpt), :], xbuf.at[buf, pl.ds(0, rpt), :], gsem.at[buf])

    def scatter_one(buf):
        return pltpu.make_async_copy(ybuf.at[buf, pl.ds(0, rpt), :], ys_ref.at[pl.ds(0, rpt), :], ssem.at[buf])

    def on_buffer(which, fn):
        for buf in (0, 1):
            @pl.when(which == buf)
            def _():
                fn(buf)

    def start_rows(n, row_copy, priority=0):
        @pl.when(n == tb)
        def _():
            for r in range(tb):
                row_copy(r).start(priority=priority)

        @pl.when(n < tb)
        def _():
            def body(r, carry):
                row_copy(r).start(priority=priority)
                return carry
            lax.fori_loop(0, n, body, 0)

    def wait_rows(n, all_copy, row_copy):
        @pl.when(n == tb)
        def _():
            all_copy.wait()

        @pl.when(n < tb)
        def _():
            def body(r, carry):
                row_copy.wait()
                return carry
            lax.fori_loop(0, n, body, 0)

    def slots_copy(blk):
        which = lax.rem(blk, SLOT_RING)
        return pltpu.make_async_copy(inv_ref.at[jnp.minimum(blk, n_blocks - 1), 0],
                                     slots.at[pl.ds(which * tb, tb)], isem.at[which])

    def nv_of(blk):
        return jnp.where((blk >= 0) & (blk < n_blocks), nv_ref[jnp.clip(blk, 0, n_blocks - 1)], 0)

    nv_next = nv_of(i + 1)

    @pl.when(i == 0)
    def _():
        xbuf[...] = jnp.zeros_like(xbuf)
        slots_copy(0).start()
        slots_copy(0).wait()
        start_rows(nv, lambda r: gather_row(0, 0, r))

        @pl.when(nv_next > 0)
        def _():
            slots_copy(1).start()

    def wait_inputs(buf):
        wait_rows(nv, gather_all(buf), gather_one(buf))
        wait_rows(nv_of(i - 2), scatter_all(buf), scatter_one(buf))

    on_buffer(b, wait_inputs)

    @pl.when(nv_of(i + 2) > 0)
    def _():
        slots_copy(i + 2).start()

    fused_issue = (nv == tb) & (nv_next == tb)

    def prefetch_next(buf):
        slots_copy(i + 1).wait()
        base = ring(i + 1)

        @pl.when(jnp.logical_not(fused_issue))
        def _():
            start_rows(nv_next, lambda r: gather_row(buf, base, r))

    @pl.when(nv_next > 0)
    def _():
        on_buffer(1 - b, prefetch_next)

    @pl.when((i == 0) | (be_ref[i] != be_ref[jnp.maximum(i - 1, 0)]))
    def _():
        wg_s[...] = wg_ref[0].astype(BF16)
        wu_s[...] = wu_ref[0].astype(BF16)
        wd_s[...] = wd_ref[0].astype(BF16)

    def ffn(issue_next_gathers):
        xb = jnp.concatenate([xbuf[b, pl.ds(j, tb, stride=rpt), :] for j in range(rpt)], axis=1).astype(BF16)
        y = None
        n_chunks = f // fc
        base_next = ring(i + 1)
        for j in range(n_chunks):
            cols = slice(j * fc, (j + 1) * fc)
            gate = jnp.minimum(jnp.dot(xb, wg_s[:, cols], preferred_element_type=F32) + bg_ref[0, :, cols],
                               SWIGLU_LIMIT)
            up = jnp.clip(jnp.dot(xb, wu_s[:, cols], preferred_element_type=F32) + bu_ref[0, :, cols],
                          -SWIGLU_LIMIT, SWIGLU_LIMIT)
            if issue_next_gathers:
                for r in range(j * tb // n_chunks, (j + 1) * tb // n_chunks):
                    gather_row(1 - b, base_next, r).start()
            act = ((up + 1.0) * (gate * jax.nn.sigmoid(SWIGLU_ALPHA * gate))).astype(BF16)
            part = jnp.dot(act, wd_s[cols, :], preferred_element_type=F32)
            y = part if y is None else y + part
        y = y + bd_ref[0]
        for j in range(rpt):
            ybuf[b, pl.ds(j, tb, stride=rpt), :] = y[:, j * LANES:(j + 1) * LANES]

    @pl.when(fused_issue)
    def _():
        ffn(True)

    @pl.when((nv > 0) & jnp.logical_not(fused_issue))
    def _():
        ffn(False)

    def send_outputs(buf):
        base = ring(i)
        start_rows(nv, lambda r: scatter_row(buf, base, r), priority=SCATTER_DMA_PRIORITY)

    on_buffer(b, send_outputs)

    @pl.when(i == n_blocks - 1)
    def _():
        on_buffer(1 - b, lambda buf: wait_rows(nv_of(i - 1), scatter_all(buf), scatter_one(buf)))
        on_buffer(b, lambda buf: wait_rows(nv, scatter_all(buf), scatter_one(buf)))


def _moe(block_e, block_nv, inv, h2t, w_gate, b_gate, w_up, b_up, w_down, b_down, *, tb, fc):
    e, d, f = w_gate.shape
    rpt = d // LANES
    n_tok = h2t.shape[0] // rpt
    n_blocks = block_nv.shape[0]
    assert n_tok & (n_tok - 1) == 0, "token count must be a power of two (slot -> token by masking)"
    assert d == f, "expert width equals model width in this block"
    wspec = lambda shape: pl.BlockSpec((1,) + shape, lambda i, be, nv: (be[i], 0, 0))
    return pl.pallas_call(
        functools.partial(_moe_kernel, tb=tb, fc=fc, n_blocks=n_blocks, n_tok=n_tok),
        grid_spec=pltpu.PrefetchScalarGridSpec(
            num_scalar_prefetch=2,
            grid=(n_blocks,),
            in_specs=[pl.BlockSpec(memory_space=pl.ANY), pl.BlockSpec(memory_space=pl.ANY),
                      wspec((d, f)), wspec((1, f)), wspec((d, f)), wspec((1, f)),
                      wspec((f, d)), wspec((1, d))],
            out_specs=pl.BlockSpec(memory_space=pl.ANY),
            scratch_shapes=[pltpu.VMEM((2, tb * rpt, LANES), F32), pltpu.VMEM((2, tb * rpt, LANES), F32),
                            pltpu.SMEM((SLOT_RING * tb,), jnp.int32),
                            pltpu.VMEM((d, f), BF16), pltpu.VMEM((d, f), BF16), pltpu.VMEM((f, d), BF16),
                            pltpu.SemaphoreType.DMA((2,)), pltpu.SemaphoreType.DMA((2,)),
                            pltpu.SemaphoreType.DMA((SLOT_RING,))]),
        out_shape=jax.ShapeDtypeStruct((TOP_K * n_tok * rpt, LANES), F32),
        compiler_params=pltpu.CompilerParams(dimension_semantics=("arbitrary",),
                                             vmem_limit_bytes=VMEM_LIMIT),
        name="moe",
    )(block_e, block_nv, inv, h2t, w_gate, b_gate.reshape(e, 1, f), w_up, b_up.reshape(e, 1, f),
      w_down, b_down.reshape(e, 1, d))


def _combine_kernel(ys_ref, x1_ref, tw_ref, mod_ref, fg_ref, o_ref):
    tw = tw_ref[...]
    tc, d = x1_ref.shape
    rpt = d // LANES

    def expert_rows(k):
        return jnp.concatenate([ys_ref[k, pl.ds(j, tc, stride=rpt), :] for j in range(rpt)], axis=1)

    moe = tw[:, 0:1] * expert_rows(0)
    for k in range(1, TOP_K):
        moe = moe + tw[:, k:k + 1] * expert_rows(k)
    xo = x1_ref[...] + mod_ref[0, 5:6, :] * moe
    o_ref[...] = _rms(xo) * fg_ref[...]


def _combine(ys4, x1, top_w, mod3, final_g, *, tc, seq):
    n_tok, d = x1.shape
    tiles_per_batch = seq // tc
    return pl.pallas_call(
        _combine_kernel,
        grid=(n_tok // tc,),
        in_specs=[pl.BlockSpec((TOP_K, tc * d // LANES, LANES), lambda i: (0, i, 0)),
                  pl.BlockSpec((tc, d), lambda i: (i, 0)),
                  pl.BlockSpec((tc, TOP_K), lambda i: (i, 0)),
                  pl.BlockSpec((1, 6, d), lambda i: (i // tiles_per_batch, 0, 0)),
                  pl.BlockSpec((1, d), lambda i: (0, 0))],
        out_specs=pl.BlockSpec((tc, d), lambda i: (i, 0)),
        out_shape=jax.ShapeDtypeStruct((n_tok, d), F32),
        compiler_params=pltpu.CompilerParams(dimension_semantics=("arbitrary",),
                                             vmem_limit_bytes=VMEM_LIMIT),
        name="combine",
    )(ys4, x1, top_w, mod3, final_g)


def _forward(x, c, ada_w, ada_b, norm1_g, w_in, lam_re, lam_im, log_dt, b_re, b_im, c_re, c_im, d_skip,
             w_glu, b_glu, conv_w, conv_b, ln_g, ln_b, out_norm_g, w_out, norm2_g, router_w, router_b,
             w_gate, b_gate, w_up, b_up, w_down, b_down, final_g, *, tm, rc, tb, fc, tc):
    bsz, seq, d = x.shape
    n_tok = bsz * seq
    cpb = seq // CHUNK
    n_chunks = n_tok // CHUNK
    row = lambda v: v.reshape(1, -1).astype(F32)

    mod3 = _mod(c, ada_w, ada_b).reshape(bsz, 6, d)

    wu_t = w_in[:, :D_SSM].T.astype(BF16)
    wvg = w_in[:, D_SSM:].astype(BF16)
    cw = jnp.zeros((CONV_HALO, D_CONV), F32).at[:CONV_WIDTH].set(conv_w.astype(F32))
    ut, nconv = _inproj(x, mod3, row(norm1_g), wu_t, wvg, cw, row(conv_b), row(ln_g), row(ln_b),
                        row(out_norm_g[D_SSM:]), tm=tm, rc=rc)

    klag, wst, vout, pq = _ssm_params(lam_re, lam_im, log_dt, b_re, b_im, c_re, c_im, cpb)
    dsk = jnp.broadcast_to(d_skip.astype(F32).reshape(N_GROUPS, SSM_GROUP, 1, 1),
                           (N_GROUPS, SSM_GROUP, 1, CHUNK))
    y4 = _ssm(ut.reshape(N_GROUPS, SSM_GROUP, n_chunks, CHUNK), klag, wst, vout, pq, dsk, cpb=cpb)
    yt = y4.reshape(D_SSM, n_chunks, CHUNK)

    rw = router_w.astype(F32)
    rw_hi = rw.astype(BF16)
    rw_split = jnp.concatenate([rw_hi, (rw - rw_hi.astype(F32)).astype(BF16)], axis=1).T
    x1, h2, top_idx, top_w = _mix(
        x, yt, nconv, mod3, w_glu.T.astype(BF16), b_glu.reshape(D_SSM, 1).astype(F32),
        out_norm_g[:D_SSM].reshape(D_SSM, 1).astype(F32), w_out[:D_SSM].astype(BF16),
        w_out[D_SSM:].astype(BF16), row(norm2_g), rw_split, router_b.reshape(N_EXPERTS, 1).astype(F32), tm=tm)

    n_assign = n_tok * TOP_K
    n_blocks = n_assign // tb + N_EXPERTS
    assert n_assign & (n_assign - 1) == 0 and (N_EXPERTS + 1) * n_assign < 2 ** 31
    experts = jnp.arange(N_EXPERTS, dtype=jnp.int32)
    counts = jnp.sum((top_idx[None] == experts[:, None, None]).astype(jnp.int32), axis=(1, 2))
    padded = (counts + tb - 1) // tb * tb
    pend = jnp.cumsum(padded)
    pstart = pend - padded
    blk_start = jnp.arange(n_blocks, dtype=jnp.int32) * tb
    block_e = jnp.minimum(jnp.sum((blk_start[:, None] >= pend[None, :]).astype(jnp.int32), axis=1), N_EXPERTS - 1)
    block_nv = jnp.clip(pstart[block_e] + counts[block_e] - blk_start, 0, tb).astype(jnp.int32)
    block_nv = jnp.where(blk_start < pend[-1], block_nv, 0)

    slot = jnp.arange(n_assign, dtype=jnp.int32).reshape(TOP_K, n_tok)
    pad_end = jnp.cumsum(padded - counts)
    pad_id = jnp.arange(n_blocks * tb - n_assign, dtype=jnp.int32)
    pad_e = jnp.sum((pad_id[:, None] >= pad_end[None, :]).astype(jnp.int32), axis=1)
    keys = jnp.concatenate([(top_idx * n_assign + slot).reshape(-1), pad_e * n_assign + (n_assign - 1)])
    inv = (lax.sort(keys) & (n_assign - 1)).reshape(n_blocks, 1, tb)

    ys4 = _moe(block_e, block_nv, inv, h2, w_gate, b_gate, w_up, b_up, w_down, b_down, tb=tb, fc=fc)
    out = _combine(ys4.reshape(TOP_K, n_tok * d // LANES, LANES), x1.reshape(n_tok, d), top_w, mod3,
                   row(final_g), tc=tc, seq=seq)
    return out.reshape(bsz, seq, d)


def kernel(x, c, ada_w, ada_b, norm1_g, w_in, lam_re, lam_im, log_dt, b_re, b_im, c_re, c_im, d_skip, w_glu, b_glu, conv_w, conv_b, ln_g, ln_b, out_norm_g, w_out, norm2_g, router_w, router_b, w_gate, b_gate, w_up, b_up, w_down, b_down, final_g):
    p = [a[0] for a in (ada_w, ada_b, norm1_g, w_in, lam_re, lam_im, log_dt, b_re, b_im, c_re, c_im, d_skip,
                        w_glu, b_glu, conv_w, conv_b, ln_g, ln_b, out_norm_g, w_out, norm2_g, router_w,
                        router_b, w_gate, b_gate, w_up, b_up, w_down, b_down)]
    return _forward(x, c, *p, final_g, tm=1024, rc=64, tb=512, fc=256, tc=512)
```

```python
import functools
import math

import jax
import jax.numpy as jnp
from jax import lax
from jax.experimental import pallas as pl
from jax.experimental.pallas import tpu as pltpu

F32 = jnp.float32
BF16 = jnp.bfloat16
HIGHEST = lax.Precision.HIGHEST

D_MODEL = 1024
D_SSM = 512
D_CONV = 512
SSM_GROUP = 16
N_GROUPS = 32
SSM_STATE = 64
CONV_WIDTH = 31
N_EXPERTS = 32
TOP_K = 4
SWIGLU_ALPHA = 1.702
SWIGLU_LIMIT = 7.0
RMS_EPS = 1e-6
LN_EPS = 1e-5

CHUNK = 128
CONV_HALO = 32
SUBLANES = 8
LANES = 128
ROW_DMA_PRIORITY = 1
SLOT_RING = 3
VMEM_LIMIT = 56 * 1024 * 1024


def _rms(x, eps=RMS_EPS):
    return x * lax.rsqrt(jnp.mean(x * x, axis=-1, keepdims=True) + eps)


def _mod_kernel(c_ref, w_ref, b_ref, o_ref):
    c = c_ref[...]
    cond = c * jax.nn.sigmoid(c)
    o_ref[...] = jnp.dot(cond, w_ref[...], precision=HIGHEST, preferred_element_type=F32) + b_ref[...]


def _mod(c, ada_w, ada_b):
    bsz, d = c.shape
    n = ada_w.shape[1]
    return pl.pallas_call(
        _mod_kernel,
        grid=(n // d,),
        in_specs=[pl.BlockSpec((bsz, d), lambda j: (0, 0)),
                  pl.BlockSpec((d, d), lambda j: (0, j)),
                  pl.BlockSpec((1, d), lambda j: (0, j))],
        out_specs=pl.BlockSpec((bsz, d), lambda j: (0, j)),
        out_shape=jax.ShapeDtypeStruct((bsz, n), F32),
        name="mod",
    )(c, ada_w, ada_b.reshape(1, n))


def _inproj_kernel(x_ref, mod_ref, g1_ref, wu_ref, wvg_ref, cw_ref, cb_ref, lg_ref, lb_ref, og_ref,
                   ut_ref, nc_ref, zext, zsh, *, tm, rc):
    i = pl.program_id(1)
    x = x_ref[0]
    y = _rms(x) * g1_ref[...]
    h = (y * (1.0 + mod_ref[0, 1:2, :]) + mod_ref[0, 0:1, :]).astype(BF16)
    ut = lax.dot_general(wu_ref[...], h, (((1,), (1,)), ((), ())), preferred_element_type=F32)
    for cc in range(tm // CHUNK):
        ut_ref[:, cc, :] = ut[:, cc * CHUNK:(cc + 1) * CHUNK]
    vg = jnp.dot(h, wvg_ref[...], preferred_element_type=F32)
    z = vg[:, :D_CONV] * jax.nn.sigmoid(vg[:, D_CONV:])

    @pl.when(i == 0)
    def _():
        zext[0:CONV_HALO, :] = jnp.zeros((CONV_HALO, D_CONV), F32)

    zext[CONV_HALO:CONV_HALO + tm, :] = z
    off = CONV_HALO - (CONV_WIDTH - 1)
    span = tm + CONV_HALO - SUBLANES
    for p in range(1, SUBLANES):
        zsh[p - 1, 0:span, :] = zext[p:p + span, :]
    for r0 in range(0, tm, rc):
        acc = jnp.zeros((rc, D_CONV), F32) + cb_ref[...]
        for k in range(CONV_WIDTH):
            p = (off + k) % SUBLANES
            lo = r0 + off + k - p
            tap = zext[lo:lo + rc, :] if p == 0 else zsh[p - 1, lo:lo + rc, :]
            acc = acc + cw_ref[k:k + 1, :] * tap
        mu = jnp.mean(acc, axis=-1, keepdims=True)
        xc = acc - mu
        ln = xc * lax.rsqrt(jnp.mean(xc * xc, axis=-1, keepdims=True) + LN_EPS) * lg_ref[...] + lb_ref[...]
        act = ln * jax.nn.sigmoid(ln)
        nc_ref[0, r0:r0 + rc, :] = (_rms(act) * og_ref[...]).astype(BF16)
    zext[0:CONV_HALO, :] = zext[tm:tm + CONV_HALO, :]


def _inproj(x, mod3, norm1_g, wu_t, wvg, conv_w, conv_b, ln_g, ln_b, og_conv, *, tm, rc):
    bsz, seq, d = x.shape
    nt = seq // tm
    full = lambda shape: pl.BlockSpec(shape, lambda b, i: (0,) * len(shape))
    return pl.pallas_call(
        functools.partial(_inproj_kernel, tm=tm, rc=rc),
        grid=(bsz, nt),
        in_specs=[pl.BlockSpec((1, tm, d), lambda b, i: (b, i, 0)),
                  pl.BlockSpec((1, 6, d), lambda b, i: (b, 0, 0)),
                  full((1, d)), full((D_SSM, d)), full((d, 2 * D_CONV)),
                  full((CONV_HALO, D_CONV)), full((1, D_CONV)), full((1, D_CONV)), full((1, D_CONV)),
                  full((1, D_CONV))],
        out_specs=[pl.BlockSpec((D_SSM, tm // CHUNK, CHUNK), lambda b, i: (0, b * nt + i, 0)),
                   pl.BlockSpec((1, tm, D_CONV), lambda b, i: (b, i, 0))],
        out_shape=[jax.ShapeDtypeStruct((D_SSM, bsz * seq // CHUNK, CHUNK), F32),
                   jax.ShapeDtypeStruct((bsz, seq, D_CONV), BF16)],
        scratch_shapes=[pltpu.VMEM((tm + CONV_HALO, D_CONV), F32),
                        pltpu.VMEM((SUBLANES - 1, tm + CONV_HALO, D_CONV), F32)],
        compiler_params=pltpu.CompilerParams(dimension_semantics=("arbitrary", "arbitrary"),
                                             vmem_limit_bytes=VMEM_LIMIT),
        name="inproj",
    )(x, mod3, norm1_g, wu_t, wvg, conv_w, conv_b, ln_g, ln_b, og_conv)


def _ssm_kernel(u_ref, klag_ref, klag_next_ref, w_ref, v_ref, pq_ref, dsk_ref, y_ref, toep_even, toep_odd, *, cpb):
    s_idx = lax.broadcasted_iota(jnp.int32, (CHUNK, CHUNK), 0)
    j_idx = lax.broadcasted_iota(jnp.int32, (CHUNK, CHUNK), 1)
    causal = j_idx >= s_idx

    def expand(lag_ref, toep):
        for hp in range(SSM_GROUP):
            kv = lag_ref[0, hp]
            blocks = []
            for h in range(SSM_GROUP):
                rows = jnp.broadcast_to(kv[h:h + 1, :], (CHUNK, CHUNK))
                skew = pltpu.roll(rows, 0, 1, stride=1, stride_axis=0)
                blocks.append(jnp.where(causal, skew, 0.0).astype(BF16))
            toep[hp * CHUNK:(hp + 1) * CHUNK, :] = jnp.concatenate(blocks, axis=1)

    g = pl.program_id(0)

    @pl.when(g == 0)
    def _():
        expand(klag_ref, toep_even)

    @pl.when(g % 2 == 0)
    def _():
        expand(klag_next_ref, toep_odd)
        _ssm_group(u_ref, toep_even, w_ref, v_ref, pq_ref, dsk_ref, y_ref, cpb=cpb)

    @pl.when(g % 2 == 1)
    def _():
        expand(klag_next_ref, toep_even)
        _ssm_group(u_ref, toep_odd, w_ref, v_ref, pq_ref, dsk_ref, y_ref, cpb=cpb)


def _ssm_group(u_ref, toep, w_ref, v_ref, pq_ref, dsk_ref, y_ref, *, cpb):
    nc = u_ref.shape[2]
    us = [u_ref[0, h] for h in range(SSM_GROUP)]
    xcat = jnp.concatenate([u.astype(BF16) for u in us], axis=1)
    acc = jnp.dot(xcat, toep[...], preferred_element_type=F32)
    st = jnp.dot(xcat, w_ref[0], preferred_element_type=F32)
    row = lax.broadcasted_iota(jnp.int32, (nc, 2 * SSM_STATE), 0) % cpb
    d = 1
    step = 0
    while d < cpb:
        sh = jnp.where(row >= d, pltpu.roll(st, d, axis=0), 0.0)
        st = st + pq_ref[0, step, 0:1, :] * sh + pq_ref[0, step, 1:2, :] * pltpu.roll(sh, SSM_STATE, axis=1)
        d *= 2
        step += 1
    prev = jnp.where(row >= 1, pltpu.roll(st, 1, axis=0), 0.0)
    acc = acc + jnp.dot(prev.astype(BF16), v_ref[0], preferred_element_type=F32)
    for h in range(SSM_GROUP):
        y_ref[0, h] = acc[:, h * CHUNK:(h + 1) * CHUNK] + dsk_ref[0, h] * us[h]


def _ssm(u4, klag, wst, vout, pq, dsk, *, cpb):
    g, hh, nc, t = u4.shape
    blk = lambda shape: pl.BlockSpec((1,) + shape, lambda i: (i,) + (0,) * len(shape))
    return pl.pallas_call(
        functools.partial(_ssm_kernel, cpb=cpb),
        grid=(g,),
        in_specs=[blk((hh, nc, t)), blk(klag.shape[1:]),
                  pl.BlockSpec((1,) + klag.shape[1:], lambda i: (jnp.minimum(i + 1, g - 1), 0, 0, 0)),
                  blk(wst.shape[1:]), blk(vout.shape[1:]), blk(pq.shape[1:]), blk(dsk.shape[1:])],
        out_specs=blk((hh, nc, t)),
        out_shape=jax.ShapeDtypeStruct(u4.shape, F32),
        scratch_shapes=[pltpu.VMEM((hh * t, hh * t), BF16), pltpu.VMEM((hh * t, hh * t), BF16)],
        compiler_params=pltpu.CompilerParams(dimension_semantics=("arbitrary",),
                                             vmem_limit_bytes=VMEM_LIMIT),
        name="ssm",
    )(u4, klag, klag, wst, vout, pq, dsk)


def _ssm_params(lam_re, lam_im, log_dt, b_re, b_im, c_re, c_im, cpb):
    lr, li = lam_re.astype(F32), lam_im.astype(F32)
    dt = jnp.exp(log_dt.astype(F32))[:, None]
    mag = jnp.exp(lr * dt)
    ab_re = mag * jnp.cos(li * dt)
    ab_im = mag * jnp.sin(li * dt)
    den = lr * lr + li * li
    nr = ab_re - 1.0
    q_re = (nr * lr + ab_im * li) / den
    q_im = (ab_im * lr - nr * li) / den
    br, bi = b_re.astype(F32), b_im.astype(F32)
    bb_re = q_re[..., None] * br - q_im[..., None] * bi
    bb_im = q_re[..., None] * bi + q_im[..., None] * br
    cr, ci = c_re.astype(F32), c_im.astype(F32)

    pr, pi = jnp.ones((1,) + ab_re.shape, F32), jnp.zeros((1,) + ab_re.shape, F32)
    cur_r, cur_i = ab_re, ab_im
    while pr.shape[0] < CHUNK:
        pr, pi = (jnp.concatenate([pr, pr * cur_r - pi * cur_i], axis=0),
                  jnp.concatenate([pi, pr * cur_i + pi * cur_r], axis=0))
        cur_r, cur_i = cur_r * cur_r - cur_i * cur_i, 2.0 * cur_r * cur_i
    cp_re = cr[None] * pr[:, :, None, :] - ci[None] * pi[:, :, None, :]
    cp_im = cr[None] * pi[:, :, None, :] + ci[None] * pr[:, :, None, :]
    klag = (jnp.einsum('lghp,gpk->gkhl', cp_re, bb_re, precision=HIGHEST)
            - jnp.einsum('lghp,gpk->gkhl', cp_im, bb_im, precision=HIGHEST))
    rr, ri = pr[::-1], pi[::-1]
    w_re = rr[..., None] * bb_re[None] - ri[..., None] * bb_im[None]
    w_im = rr[..., None] * bb_im[None] + ri[..., None] * bb_re[None]
    wst = jnp.concatenate([w_re, w_im], axis=2).transpose(1, 3, 0, 2)
    wst = wst.reshape(N_GROUPS, SSM_GROUP * CHUNK, 2 * SSM_STATE).astype(BF16)
    p1r, p1i = pr * ab_re - pi * ab_im, pr * ab_im + pi * ab_re
    v_re = cr[None] * p1r[:, :, None, :] - ci[None] * p1i[:, :, None, :]
    v_im = cr[None] * p1i[:, :, None, :] + ci[None] * p1r[:, :, None, :]
    vout = jnp.concatenate([v_re, -v_im], axis=3).transpose(1, 3, 2, 0)
    vout = vout.reshape(N_GROUPS, 2 * SSM_STATE, SSM_GROUP * CHUNK).astype(BF16)
    tabs = []
    d = 1
    while d < cpb or not tabs:
        tabs.append(jnp.stack([jnp.concatenate([cur_r, cur_r], axis=1),
                               jnp.concatenate([-cur_i, cur_i], axis=1)], axis=1))
        cur_r, cur_i = cur_r * cur_r - cur_i * cur_i, 2.0 * cur_r * cur_i
        d *= 2
    pq = jnp.stack(tabs, axis=1)
    return klag, wst, vout, pq


def _mix_kernel(x_ref, yt_ref, nc_ref, mod_ref, wglu_ref, bglu_ref, ogs_ref, wot_ref, wob_ref, g2_ref,
                rw_ref, rb_ref, x1_ref, h2_ref, idx_ref, tw_ref):
    yt = jnp.concatenate([yt_ref[:, cc, :] for cc in range(yt_ref.shape[1])], axis=1)
    yg = 0.5 * yt * (1.0 + jnp.tanh(math.sqrt(2.0 / math.pi) * (yt + 0.044715 * (yt * yt * yt))))
    gate = jnp.dot(wglu_ref[...], yg.astype(BF16), preferred_element_type=F32) + bglu_ref[...]
    y2 = yg * jax.nn.sigmoid(gate)
    ms = jnp.mean(y2 * y2, axis=0, keepdims=True)
    ns = (y2 * lax.rsqrt(ms + RMS_EPS) * ogs_ref[...]).astype(BF16)
    o = (lax.dot_general(ns, wot_ref[...], (((0,), (0,)), ((), ())), preferred_element_type=F32)
         + jnp.dot(nc_ref[0], wob_ref[...], preferred_element_type=F32))
    x1 = x_ref[0] + mod_ref[0, 2:3, :] * o
    x1_ref[0] = x1
    h2 = _rms(x1) * g2_ref[...] * (1.0 + mod_ref[0, 4:5, :]) + mod_ref[0, 3:4, :]
    rpt = h2.shape[1] // LANES
    for j in range(rpt):
        h2_ref[pl.ds(j, h2.shape[0], stride=rpt), :] = h2[:, j * LANES:(j + 1) * LANES]
    nt_dims = (((1,), (1,)), ((), ()))
    h2_hi = h2.astype(BF16)
    h2_lo = (h2 - h2_hi.astype(F32)).astype(BF16)
    l_hi = lax.dot_general(rw_ref[...], h2_hi, nt_dims, preferred_element_type=F32)
    l_lo = lax.dot_general(rw_ref[:N_EXPERTS, :], h2_lo, nt_dims, preferred_element_type=F32)
    logits = l_hi[:N_EXPERTS, :] + (l_hi[N_EXPERTS:, :] + l_lo) + rb_ref[...]
    expert = lax.broadcasted_iota(jnp.int32, logits.shape, 0)
    vals, idxs = [], []
    for _ in range(TOP_K):
        m = jnp.max(logits, axis=0, keepdims=True)
        sel = jnp.min(jnp.where(logits == m, expert, N_EXPERTS), axis=0, keepdims=True)
        vals.append(m)
        idxs.append(sel)
        logits = jnp.where(expert == sel, -jnp.inf, logits)
    es = [jnp.exp(v - vals[0]) for v in vals]
    tot = es[0] + es[1] + es[2] + es[3]
    idx_ref[...] = jnp.concatenate(idxs, axis=0)
    tw_rows = jnp.concatenate([e / tot for e in es] + [jnp.zeros((LANES - TOP_K, h2.shape[0]), F32)], axis=0)
    tw_ref[...] = tw_rows.T[:, :TOP_K]


def _mix(x, yt, nconv, mod3, wglu_t, bglu, og_ssm, wo_top, wo_bot, norm2_g, router_w, router_b, *, tm):
    bsz, seq, d = x.shape
    nt = seq // tm
    n_tok = bsz * seq
    full = lambda shape: pl.BlockSpec(shape, lambda b, i: (0,) * len(shape))
    return pl.pallas_call(
        _mix_kernel,
        grid=(bsz, nt),
        in_specs=[pl.BlockSpec((1, tm, d), lambda b, i: (b, i, 0)),
                  pl.BlockSpec((D_SSM, tm // CHUNK, CHUNK), lambda b, i: (0, b * nt + i, 0)),
                  pl.BlockSpec((1, tm, D_CONV), lambda b, i: (b, i, 0)),
                  pl.BlockSpec((1, 6, d), lambda b, i: (b, 0, 0)),
                  full((D_SSM, D_SSM)), full((D_SSM, 1)), full((D_SSM, 1)),
                  full((D_SSM, d)), full((D_CONV, d)), full((1, d)),
                  full((2 * N_EXPERTS, d)), full((N_EXPERTS, 1))],
        out_specs=[pl.BlockSpec((1, tm, d), lambda b, i: (b, i, 0)),
                   pl.BlockSpec((tm * d // LANES, LANES), lambda b, i: (b * nt + i, 0)),
                   pl.BlockSpec((TOP_K, tm), lambda b, i: (0, b * nt + i)),
                   pl.BlockSpec((tm, TOP_K), lambda b, i: (b * nt + i, 0))],
        out_shape=[jax.ShapeDtypeStruct((bsz, seq, d), F32),
                   jax.ShapeDtypeStruct((n_tok * d // LANES, LANES), F32),
                   jax.ShapeDtypeStruct((TOP_K, n_tok), jnp.int32),
                   jax.ShapeDtypeStruct((n_tok, TOP_K), F32)],
        compiler_params=pltpu.CompilerParams(dimension_semantics=("arbitrary", "arbitrary"),
                                             vmem_limit_bytes=VMEM_LIMIT),
        name="mix",
    )(x, yt, nconv, mod3, wglu_t, bglu, og_ssm, wo_top, wo_bot, norm2_g, router_w, router_b)


def _moe_kernel(be_ref, nv_ref, inv_ref, h_ref, wg_ref, bg_ref, wu_ref, bu_ref, wd_ref, bd_ref,
                ys_ref, xbuf, ybuf, slots, wg_s, wu_s, wd_s, gsem, ssem, isem,
                *, tb, fc, n_blocks, n_tok):
    i = pl.program_id(0)
    b = i % 2
    nv = nv_ref[i]
    f = wg_s.shape[1]
    d = f
    rpt = d // LANES

    def ring(blk):
        return lax.rem(blk, SLOT_RING) * tb

    def gather_row(buf, base, r):
        tok = slots[base + r] & (n_tok - 1)
        return pltpu.make_async_copy(h_ref.at[pl.ds(pl.multiple_of(tok * rpt, rpt), rpt), :],
                                     xbuf.at[buf, pl.ds(pl.multiple_of(r * rpt, rpt), rpt), :], gsem.at[buf])

    def scatter_row(buf, base, r):
        return pltpu.make_async_copy(ybuf.at[buf, pl.ds(pl.multiple_of(r * rpt, rpt), rpt), :],
                                     ys_ref.at[pl.ds(pl.multiple_of(slots[base + r] * rpt, rpt), rpt), :],
                                     ssem.at[buf])

    def gather_all(buf):
        return pltpu.make_async_copy(h_ref.at[pl.ds(0, tb * rpt), :], xbuf.at[buf], gsem.at[buf])

    def scatter_all(buf):
        return pltpu.make_async_copy(ybuf.at[buf], ys_ref.at[pl.ds(0, tb * rpt), :], ssem.at[buf])

    def gather_one(buf):
        return pltpu.make_async_copy(h_ref.at[pl.ds(0, rpt), :], xbuf.at[buf, pl.ds(0, rpt), :], gsem.at[buf])

    def scatter_one(buf):
        return pltpu.make_async_copy(ybuf.at[buf, pl.ds(0, rpt), :], ys_ref.at[pl.ds(0, rpt), :], ssem.at[buf])

    def on_buffer(which, fn):
        for buf in (0, 1):
            @pl.when(which == buf)
            def _():
                fn(buf)

    def start_rows(n, row_copy, priority=0):
        @pl.when(n == tb)
        def _():
            for r in range(tb):
                row_copy(r).start(priority=priority)

        @pl.when(n < tb)
        def _():
            def body(r, carry):
                row_copy(r).start(priority=priority)
                return carry
            lax.fori_loop(0, n, body, 0)

    def wait_rows(n, all_copy, row_copy):
        @pl.when(n == tb)
        def _():
            all_copy.wait()

        @pl.when(n < tb)
        def _():
            def body(r, carry):
                row_copy.wait()
                return carry
            lax.fori_loop(0, n, body, 0)

    def slots_copy(blk):
        which = lax.rem(blk, SLOT_RING)
        return pltpu.make_async_copy(inv_ref.at[jnp.minimum(blk, n_blocks - 1), 0],
                                     slots.at[pl.ds(which * tb, tb)], isem.at[which])

    def nv_of(blk):
        return jnp.where((blk >= 0) & (blk < n_blocks), nv_ref[jnp.clip(blk, 0, n_blocks - 1)], 0)

    nv_next = nv_of(i + 1)

    @pl.when(i == 0)
    def _():
        xbuf[...] = jnp.zeros_like(xbuf)
        slots_copy(0).start()
        slots_copy(0).wait()
        start_rows(nv, lambda r: gather_row(0, 0, r), priority=ROW_DMA_PRIORITY)

        @pl.when(nv_next > 0)
        def _():
            slots_copy(1).start()

    def wait_inputs(buf):
        wait_rows(nv, gather_all(buf), gather_one(buf))
        wait_rows(nv_of(i - 2), scatter_all(buf), scatter_one(buf))

    on_buffer(b, wait_inputs)

    @pl.when(nv_of(i + 2) > 0)
    def _():
        slots_copy(i + 2).start()

    fused_issue = (nv == tb) & (nv_next == tb)

    def prefetch_next(buf):
        slots_copy(i + 1).wait()
        base = ring(i + 1)

        @pl.when(jnp.logical_not(fused_issue))
        def _():
            start_rows(nv_next, lambda r: gather_row(buf, base, r), priority=ROW_DMA_PRIORITY)

    @pl.when(nv_next > 0)
    def _():
        on_buffer(1 - b, prefetch_next)

    @pl.when((i == 0) | (be_ref[i] != be_ref[jnp.maximum(i - 1, 0)]))
    def _():
        wg_s[...] = wg_ref[0].astype(BF16)
        wu_s[...] = wu_ref[0].astype(BF16)
        wd_s[...] = wd_ref[0].astype(BF16)

    def ffn(issue_next_gathers):
        xb = jnp.concatenate([xbuf[b, pl.ds(j, tb, stride=rpt), :] for j in range(rpt)], axis=1).astype(BF16)
        y = None
        n_chunks = f // fc
        base_next = ring(i + 1)
        for j in range(n_chunks):
            cols = slice(j * fc, (j + 1) * fc)
            gate = jnp.minimum(jnp.dot(xb, wg_s[:, cols], preferred_element_type=F32) + bg_ref[0, :, cols],
                               SWIGLU_LIMIT)
            up = jnp.clip(jnp.dot(xb, wu_s[:, cols], preferred_element_type=F32) + bu_ref[0, :, cols],
                          -SWIGLU_LIMIT, SWIGLU_LIMIT)
            if issue_next_gathers:
                for r in range(j * tb // n_chunks, (j + 1) * tb // n_chunks):
                    gather_row(1 - b, base_next, r).start(priority=ROW_DMA_PRIORITY)
            act = ((up + 1.0) * (gate * jax.nn.sigmoid(SWIGLU_ALPHA * gate))).astype(BF16)
            part = jnp.dot(act, wd_s[cols, :], preferred_element_type=F32)
            y = part if y is None else y + part
        y = y + bd_ref[0]
        for j in range(rpt):
            ybuf[b, pl.ds(j, tb, stride=rpt), :] = y[:, j * LANES:(j + 1) * LANES]

    @pl.when(fused_issue)
    def _():
        ffn(True)

    @pl.when((nv > 0) & jnp.logical_not(fused_issue))
    def _():
        ffn(False)

    def send_outputs(buf):
        base = ring(i)
        start_rows(nv, lambda r: scatter_row(buf, base, r), priority=ROW_DMA_PRIORITY)

    on_buffer(b, send_outputs)

    @pl.when(i == n_blocks - 1)
    def _():
        on_buffer(1 - b, lambda buf: wait_rows(nv_of(i - 1), scatter_all(buf), scatter_one(buf)))
        on_buffer(b, lambda buf: wait_rows(nv, scatter_all(buf), scatter_one(buf)))


def _moe(block_e, block_nv, inv, h2t, w_gate, b_gate, w_up, b_up, w_down, b_down, *, tb, fc):
    e, d, f = w_gate.shape
    rpt = d // LANES
    n_tok = h2t.shape[0] // rpt
    n_blocks = block_nv.shape[0]
    assert n_tok & (n_tok - 1) == 0, "token count must be a power of two (slot -> token by masking)"
    assert d == f, "expert width equals model width in this block"
    wspec = lambda shape: pl.BlockSpec((1,) + shape, lambda i, be, nv: (be[i], 0, 0))
    return pl.pallas_call(
        functools.partial(_moe_kernel, tb=tb, fc=fc, n_blocks=n_blocks, n_tok=n_tok),
        grid_spec=pltpu.PrefetchScalarGridSpec(
            num_scalar_prefetch=2,
            grid=(n_blocks,),
            in_specs=[pl.BlockSpec(memory_space=pl.ANY), pl.BlockSpec(memory_space=pl.ANY),
                      wspec((d, f)), wspec((1, f)), wspec((d, f)), wspec((1, f)),
                      wspec((f, d)), wspec((1, d))],
            out_specs=pl.BlockSpec(memory_space=pl.ANY),
            scratch_shapes=[pltpu.VMEM((2, tb * rpt, LANES), F32), pltpu.VMEM((2, tb * rpt, LANES), F32),
                            pltpu.SMEM((SLOT_RING * tb,), jnp.int32),
                            pltpu.VMEM((d, f), BF16), pltpu.VMEM((d, f), BF16), pltpu.VMEM((f, d), BF16),
                            pltpu.SemaphoreType.DMA((2,)), pltpu.SemaphoreType.DMA((2,)),
                            pltpu.SemaphoreType.DMA((SLOT_RING,))]),
        out_shape=jax.ShapeDtypeStruct((TOP_K * n_tok * rpt, LANES), F32),
        compiler_params=pltpu.CompilerParams(dimension_semantics=("arbitrary",),
                                             vmem_limit_bytes=VMEM_LIMIT),
        name="moe",
    )(block_e, block_nv, inv, h2t, w_gate, b_gate.reshape(e, 1, f), w_up, b_up.reshape(e, 1, f),
      w_down, b_down.reshape(e, 1, d))


def _combine_kernel(ys_ref, x1_ref, tw_ref, mod_ref, fg_ref, o_ref):
    tw = tw_ref[...]
    tc, d = x1_ref.shape
    rpt = d // LANES

    def expert_rows(k):
        return jnp.concatenate([ys_ref[k, pl.ds(j, tc, stride=rpt), :] for j in range(rpt)], axis=1)

    moe = tw[:, 0:1] * expert_rows(0)
    for k in range(1, TOP_K):
        moe = moe + tw[:, k:k + 1] * expert_rows(k)
    xo = x1_ref[...] + mod_ref[0, 5:6, :] * moe
    o_ref[...] = _rms(xo) * fg_ref[...]


def _combine(ys4, x1, top_w, mod3, final_g, *, tc, seq):
    n_tok, d = x1.shape
    tiles_per_batch = seq // tc
    return pl.pallas_call(
        _combine_kernel,
        grid=(n_tok // tc,),
        in_specs=[pl.BlockSpec((TOP_K, tc * d // LANES, LANES), lambda i: (0, i, 0)),
                  pl.BlockSpec((tc, d), lambda i: (i, 0)),
                  pl.BlockSpec((tc, TOP_K), lambda i: (i, 0)),
                  pl.BlockSpec((1, 6, d), lambda i: (i // tiles_per_batch, 0, 0)),
                  pl.BlockSpec((1, d), lambda i: (0, 0))],
        out_specs=pl.BlockSpec((tc, d), lambda i: (i, 0)),
        out_shape=jax.ShapeDtypeStruct((n_tok, d), F32),
        compiler_params=pltpu.CompilerParams(dimension_semantics=("arbitrary",),
                                             vmem_limit_bytes=VMEM_LIMIT),
        name="combine",
    )(ys4, x1, top_w, mod3, final_g)


def _forward(x, c, ada_w, ada_b, norm1_g, w_in, lam_re, lam_im, log_dt, b_re, b_im, c_re, c_im, d_skip,
             w_glu, b_glu, conv_w, conv_b, ln_g, ln_b, out_norm_g, w_out, norm2_g, router_w, router_b,
             w_gate, b_gate, w_up, b_up, w_down, b_down, final_g, *, tm, rc, tb, fc, tc):
    bsz, seq, d = x.shape
    n_tok = bsz * seq
    cpb = seq // CHUNK
    n_chunks = n_tok // CHUNK
    row = lambda v: v.reshape(1, -1).astype(F32)

    mod3 = _mod(c, ada_w, ada_b).reshape(bsz, 6, d)

    wu_t = w_in[:, :D_SSM].T.astype(BF16)
    wvg = w_in[:, D_SSM:].astype(BF16)
    cw = jnp.zeros((CONV_HALO, D_CONV), F32).at[:CONV_WIDTH].set(conv_w.astype(F32))
    ut, nconv = _inproj(x, mod3, row(norm1_g), wu_t, wvg, cw, row(conv_b), row(ln_g), row(ln_b),
                        row(out_norm_g[D_SSM:]), tm=tm, rc=rc)

    klag, wst, vout, pq = _ssm_params(lam_re, lam_im, log_dt, b_re, b_im, c_re, c_im, cpb)
    dsk = jnp.broadcast_to(d_skip.astype(F32).reshape(N_GROUPS, SSM_GROUP, 1, 1),
                           (N_GROUPS, SSM_GROUP, 1, CHUNK))
    y4 = _ssm(ut.reshape(N_GROUPS, SSM_GROUP, n_chunks, CHUNK), klag, wst, vout, pq, dsk, cpb=cpb)
    yt = y4.reshape(D_SSM, n_chunks, CHUNK)

    rw = router_w.astype(F32)
    rw_hi = rw.astype(BF16)
    rw_split = jnp.concatenate([rw_hi, (rw - rw_hi.astype(F32)).astype(BF16)], axis=1).T
    x1, h2, top_idx, top_w = _mix(
        x, yt, nconv, mod3, w_glu.T.astype(BF16), b_glu.reshape(D_SSM, 1).astype(F32),
        out_norm_g[:D_SSM].reshape(D_SSM, 1).astype(F32), w_out[:D_SSM].astype(BF16),
        w_out[D_SSM:].astype(BF16), row(norm2_g), rw_split, router_b.reshape(N_EXPERTS, 1).astype(F32), tm=tm)

    n_assign = n_tok * TOP_K
    n_blocks = n_assign // tb + N_EXPERTS
    assert n_assign & (n_assign - 1) == 0 and (N_EXPERTS + 1) * n_assign < 2 ** 31
    experts = jnp.arange(N_EXPERTS, dtype=jnp.int32)
    counts = jnp.sum((top_idx[None] == experts[:, None, None]).astype(jnp.int32), axis=(1, 2))
    padded = (counts + tb - 1) // tb * tb
    pend = jnp.cumsum(padded)
    pstart = pend - padded
    blk_start = jnp.arange(n_blocks, dtype=jnp.int32) * tb
    block_e = jnp.minimum(jnp.sum((blk_start[:, None] >= pend[None, :]).astype(jnp.int32), axis=1), N_EXPERTS - 1)
    block_nv = jnp.clip(pstart[block_e] + counts[block_e] - blk_start, 0, tb).astype(jnp.int32)
    block_nv = jnp.where(blk_start < pend[-1], block_nv, 0)

    slot = jnp.arange(n_assign, dtype=jnp.int32).reshape(TOP_K, n_tok)
    pad_end = jnp.cumsum(padded - counts)
    pad_id = jnp.arange(n_blocks * tb - n_assign, dtype=jnp.int32)
    pad_e = jnp.sum((pad_id[:, None] >= pad_end[None, :]).astype(jnp.int32), axis=1)
    keys = jnp.concatenate([(top_idx * n_assign + slot).reshape(-1), pad_e * n_assign + (n_assign - 1)])
    inv = (lax.sort(keys) & (n_assign - 1)).reshape(n_blocks, 1, tb)

    ys4 = _moe(block_e, block_nv, inv, h2, w_gate, b_gate, w_up, b_up, w_down, b_down, tb=tb, fc=fc)
    out = _combine(ys4.reshape(TOP_K, n_tok * d // LANES, LANES), x1.reshape(n_tok, d), top_w, mod3,
                   row(final_g), tc=tc, seq=seq)
    return out.reshape(bsz, seq, d)


def kernel(x, c, ada_w, ada_b, norm1_g, w_in, lam_re, lam_im, log_dt, b_re, b_im, c_re, c_im, d_skip, w_glu, b_glu, conv_w, conv_b, ln_g, ln_b, out_norm_g, w_out, norm2_g, router_w, router_b, w_gate, b_gate, w_up, b_up, w_down, b_down, final_g):
    p = [a[0] for a in (ada_w, ada_b, norm1_g, w_in, lam_re, lam_im, log_dt, b_re, b_im, c_re, c_im, d_skip,
                        w_glu, b_glu, conv_w, conv_b, ln_g, ln_b, out_norm_g, w_out, norm2_g, router_w,
                        router_b, w_gate, b_gate, w_up, b_up, w_down, b_down)]
    return _forward(x, c, *p, final_g, tm=1024, rc=64, tb=512, fc=256, tc=512)
```

```python
import functools
import math

import jax
import jax.numpy as jnp
from jax import lax
from jax.experimental import pallas as pl
from jax.experimental.pallas import tpu as pltpu

F32 = jnp.float32
BF16 = jnp.bfloat16
HIGHEST = lax.Precision.HIGHEST

D_MODEL = 1024
D_SSM = 512
D_CONV = 512
SSM_GROUP = 16
N_GROUPS = 32
SSM_STATE = 64
CONV_WIDTH = 31
N_EXPERTS = 32
TOP_K = 4
SWIGLU_ALPHA = 1.702
SWIGLU_LIMIT = 7.0
RMS_EPS = 1e-6
LN_EPS = 1e-5

CHUNK = 128
CONV_HALO = 32
SUBLANES = 8
LANES = 128
ROW_DMA_PRIORITY = 1
SLOT_RING = 3
VMEM_LIMIT = 56 * 1024 * 1024


def _rms(x, eps=RMS_EPS):
    return x * lax.rsqrt(jnp.mean(x * x, axis=-1, keepdims=True) + eps)


def _mod_kernel(c_ref, w_ref, b_ref, o_ref):
    c = c_ref[...]
    cond = c * jax.nn.sigmoid(c)
    o_ref[...] = jnp.dot(cond, w_ref[...], precision=HIGHEST, preferred_element_type=F32) + b_ref[...]


def _mod(c, ada_w, ada_b):
    bsz, d = c.shape
    n = ada_w.shape[1]
    return pl.pallas_call(
        _mod_kernel,
        grid=(n // d,),
        in_specs=[pl.BlockSpec((bsz, d), lambda j: (0, 0)),
                  pl.BlockSpec((d, d), lambda j: (0, j)),
                  pl.BlockSpec((1, d), lambda j: (0, j))],
        out_specs=pl.BlockSpec((bsz, d), lambda j: (0, j)),
        out_shape=jax.ShapeDtypeStruct((bsz, n), F32),
        name="mod",
    )(c, ada_w, ada_b.reshape(1, n))


def _inproj_kernel(x_ref, mod_ref, g1_ref, wu_ref, wvg_ref, cw_ref, cb_ref, lg_ref, lb_ref, og_ref,
                   ut_ref, nc_ref, zext, zsh, *, tm, rc):
    i = pl.program_id(1)
    x = x_ref[0]
    y = _rms(x) * g1_ref[...]
    h = (y * (1.0 + mod_ref[0, 1:2, :]) + mod_ref[0, 0:1, :]).astype(BF16)
    ut = lax.dot_general(wu_ref[...], h, (((1,), (1,)), ((), ())), preferred_element_type=F32)
    for cc in range(tm // CHUNK):
        ut_ref[:, cc, :] = ut[:, cc * CHUNK:(cc + 1) * CHUNK]
    vg = jnp.dot(h, wvg_ref[...], preferred_element_type=F32)
    z = vg[:, :D_CONV] * jax.nn.sigmoid(vg[:, D_CONV:])

    @pl.when(i == 0)
    def _():
        zext[0:CONV_HALO, :] = jnp.zeros((CONV_HALO, D_CONV), F32)

    zext[CONV_HALO:CONV_HALO + tm, :] = z
    off = CONV_HALO - (CONV_WIDTH - 1)
    span = tm + CONV_HALO - SUBLANES
    for p in range(1, SUBLANES):
        zsh[p - 1, 0:span, :] = zext[p:p + span, :]
    for r0 in range(0, tm, rc):
        acc = jnp.zeros((rc, D_CONV), F32) + cb_ref[...]
        for k in range(CONV_WIDTH):
            p = (off + k) % SUBLANES
            lo = r0 + off + k - p
            tap = zext[lo:lo + rc, :] if p == 0 else zsh[p - 1, lo:lo + rc, :]
            acc = acc + cw_ref[k:k + 1, :] * tap
        mu = jnp.mean(acc, axis=-1, keepdims=True)
        xc = acc - mu
        ln = xc * lax.rsqrt(jnp.mean(xc * xc, axis=-1, keepdims=True) + LN_EPS) * lg_ref[...] + lb_ref[...]
        act = ln * jax.nn.sigmoid(ln)
        nc_ref[0, r0:r0 + rc, :] = (_rms(act) * og_ref[...]).astype(BF16)
    zext[0:CONV_HALO, :] = zext[tm:tm + CONV_HALO, :]


def _inproj(x, mod3, norm1_g, wu_t, wvg, conv_w, conv_b, ln_g, ln_b, og_conv, *, tm, rc):
    bsz, seq, d = x.shape
    nt = seq // tm
    full = lambda shape: pl.BlockSpec(shape, lambda b, i: (0,) * len(shape))
    return pl.pallas_call(
        functools.partial(_inproj_kernel, tm=tm, rc=rc),
        grid=(bsz, nt),
        in_specs=[pl.BlockSpec((1, tm, d), lambda b, i: (b, i, 0)),
                  pl.BlockSpec((1, 6, d), lambda b, i: (b, 0, 0)),
                  full((1, d)), full((D_SSM, d)), full((d, 2 * D_CONV)),
                  full((CONV_HALO, D_CONV)), full((1, D_CONV)), full((1, D_CONV)), full((1, D_CONV)),
                  full((1, D_CONV))],
        out_specs=[pl.BlockSpec((D_SSM, tm // CHUNK, CHUNK), lambda b, i: (0, b * nt + i, 0)),
                   pl.BlockSpec((1, tm, D_CONV), lambda b, i: (b, i, 0))],
        out_shape=[jax.ShapeDtypeStruct((D_SSM, bsz * seq // CHUNK, CHUNK), F32),
                   jax.ShapeDtypeStruct((bsz, seq, D_CONV), BF16)],
        scratch_shapes=[pltpu.VMEM((tm + CONV_HALO, D_CONV), F32),
                        pltpu.VMEM((SUBLANES - 1, tm + CONV_HALO, D_CONV), F32)],
        compiler_params=pltpu.CompilerParams(dimension_semantics=("arbitrary", "arbitrary"),
                                             vmem_limit_bytes=VMEM_LIMIT),
        name="inproj",
    )(x, mod3, norm1_g, wu_t, wvg, conv_w, conv_b, ln_g, ln_b, og_conv)


def _ssm_kernel(u_ref, klag_ref, klag_next_ref, w_ref, v_ref, pq_ref, dsk_ref, y_ref, toep_even, toep_odd, *, cpb):
    s_idx = lax.broadcasted_iota(jnp.int32, (CHUNK, CHUNK), 0)
    j_idx = lax.broadcasted_iota(jnp.int32, (CHUNK, CHUNK), 1)
    causal = j_idx >= s_idx

    def expand(lag_ref, toep):
        for hp in range(SSM_GROUP):
            kv = lag_ref[0, hp]
            blocks = []
            for h in range(SSM_GROUP):
                rows = jnp.broadcast_to(kv[h:h + 1, :], (CHUNK, CHUNK))
                skew = pltpu.roll(rows, 0, 1, stride=1, stride_axis=0)
                blocks.append(jnp.where(causal, skew, 0.0).astype(BF16))
            toep[hp * CHUNK:(hp + 1) * CHUNK, :] = jnp.concatenate(blocks, axis=1)

    g = pl.program_id(0)

    @pl.when(g == 0)
    def _():
        expand(klag_ref, toep_even)

    @pl.when(g % 2 == 0)
    def _():
        expand(klag_next_ref, toep_odd)
        _ssm_group(u_ref, toep_even, w_ref, v_ref, pq_ref, dsk_ref, y_ref, cpb=cpb)

    @pl.when(g % 2 == 1)
    def _():
        expand(klag_next_ref, toep_even)
        _ssm_group(u_ref, toep_odd, w_ref, v_ref, pq_ref, dsk_ref, y_ref, cpb=cpb)


def _ssm_group(u_ref, toep, w_ref, v_ref, pq_ref, dsk_ref, y_ref, *, cpb):
    nc = u_ref.shape[2]
    us = [u_ref[0, h] for h in range(SSM_GROUP)]
    xcat = jnp.concatenate([u.astype(BF16) for u in us], axis=1)
    acc = jnp.dot(xcat, toep[...], preferred_element_type=F32)
    st = jnp.dot(xcat, w_ref[0], preferred_element_type=F32)
    row = lax.broadcasted_iota(jnp.int32, (nc, 2 * SSM_STATE), 0) % cpb
    d = 1
    step = 0
    while d < cpb:
        sh = jnp.where(row >= d, pltpu.roll(st, d, axis=0), 0.0)
        st = st + pq_ref[0, step, 0:1, :] * sh + pq_ref[0, step, 1:2, :] * pltpu.roll(sh, SSM_STATE, axis=1)
        d *= 2
        step += 1
    prev = jnp.where(row >= 1, pltpu.roll(st, 1, axis=0), 0.0)
    acc = acc + jnp.dot(prev.astype(BF16), v_ref[0], preferred_element_type=F32)
    for h in range(SSM_GROUP):
        y_ref[0, h] = acc[:, h * CHUNK:(h + 1) * CHUNK] + dsk_ref[0, h] * us[h]


def _ssm(u4, klag, wst, vout, pq, dsk, *, cpb):
    g, hh, nc, t = u4.shape
    blk = lambda shape: pl.BlockSpec((1,) + shape, lambda i: (i,) + (0,) * len(shape))
    return pl.pallas_call(
        functools.partial(_ssm_kernel, cpb=cpb),
        grid=(g,),
        in_specs=[blk((hh, nc, t)), blk(klag.shape[1:]),
                  pl.BlockSpec((1,) + klag.shape[1:], lambda i: (jnp.minimum(i + 1, g - 1), 0, 0, 0)),
                  blk(wst.shape[1:]), blk(vout.shape[1:]), blk(pq.shape[1:]), blk(dsk.shape[1:])],
        out_specs=blk((hh, nc, t)),
        out_shape=jax.ShapeDtypeStruct(u4.shape, F32),
        scratch_shapes=[pltpu.VMEM((hh * t, hh * t), BF16), pltpu.VMEM((hh * t, hh * t), BF16)],
        compiler_params=pltpu.CompilerParams(dimension_semantics=("arbitrary",),
                                             vmem_limit_bytes=VMEM_LIMIT),
        name="ssm",
    )(u4, klag, klag, wst, vout, pq, dsk)


def _ssm_params(lam_re, lam_im, log_dt, b_re, b_im, c_re, c_im, cpb):
    lr, li = lam_re.astype(F32), lam_im.astype(F32)
    dt = jnp.exp(log_dt.astype(F32))[:, None]
    mag = jnp.exp(lr * dt)
    ab_re = mag * jnp.cos(li * dt)
    ab_im = mag * jnp.sin(li * dt)
    den = lr * lr + li * li
    nr = ab_re - 1.0
    q_re = (nr * lr + ab_im * li) / den
    q_im = (ab_im * lr - nr * li) / den
    br, bi = b_re.astype(F32), b_im.astype(F32)
    bb_re = q_re[..., None] * br - q_im[..., None] * bi
    bb_im = q_re[..., None] * bi + q_im[..., None] * br
    cr, ci = c_re.astype(F32), c_im.astype(F32)

    pr, pi = jnp.ones((1,) + ab_re.shape, F32), jnp.zeros((1,) + ab_re.shape, F32)
    cur_r, cur_i = ab_re, ab_im
    while pr.shape[0] < CHUNK:
        pr, pi = (jnp.concatenate([pr, pr * cur_r - pi * cur_i], axis=0),
                  jnp.concatenate([pi, pr * cur_i + pi * cur_r], axis=0))
        cur_r, cur_i = cur_r * cur_r - cur_i * cur_i, 2.0 * cur_r * cur_i
    cp_re = cr[None] * pr[:, :, None, :] - ci[None] * pi[:, :, None, :]
    cp_im = cr[None] * pi[:, :, None, :] + ci[None] * pr[:, :, None, :]
    klag = (jnp.einsum('lghp,gpk->gkhl', cp_re, bb_re, precision=HIGHEST)
            - jnp.einsum('lghp,gpk->gkhl', cp_im, bb_im, precision=HIGHEST))
    rr, ri = pr[::-1], pi[::-1]
    w_re = rr[..., None] * bb_re[None] - ri[..., None] * bb_im[None]
    w_im = rr[..., None] * bb_im[None] + ri[..., None] * bb_re[None]
    wst = jnp.concatenate([w_re, w_im], axis=2).transpose(1, 3, 0, 2)
    wst = wst.reshape(N_GROUPS, SSM_GROUP * CHUNK, 2 * SSM_STATE).astype(BF16)
    p1r, p1i = pr * ab_re - pi * ab_im, pr * ab_im + pi * ab_re
    v_re = cr[None] * p1r[:, :, None, :] - ci[None] * p1i[:, :, None, :]
    v_im = cr[None] * p1i[:, :, None, :] + ci[None] * p1r[:, :, None, :]
    vout = jnp.concatenate([v_re, -v_im], axis=3).transpose(1, 3, 2, 0)
    vout = vout.reshape(N_GROUPS, 2 * SSM_STATE, SSM_GROUP * CHUNK).astype(BF16)
    tabs = []
    d = 1
    while d < cpb or not tabs:
        tabs.append(jnp.stack([jnp.concatenate([cur_r, cur_r], axis=1),
                               jnp.concatenate([-cur_i, cur_i], axis=1)], axis=1))
        cur_r, cur_i = cur_r * cur_r - cur_i * cur_i, 2.0 * cur_r * cur_i
        d *= 2
    pq = jnp.stack(tabs, axis=1)
    return klag, wst, vout, pq


def _mix_kernel(x_ref, yt_ref, nc_ref, mod_ref, wglu_ref, bglu_ref, ogs_ref, wot_ref, wob_ref, g2_ref,
                rw_ref, rb_ref, x1_ref, h2_ref, idx_ref, tw_ref):
    yt = jnp.concatenate([yt_ref[:, cc, :] for cc in range(yt_ref.shape[1])], axis=1)
    yg = 0.5 * yt * (1.0 + jnp.tanh(math.sqrt(2.0 / math.pi) * (yt + 0.044715 * (yt * yt * yt))))
    gate = jnp.dot(wglu_ref[...], yg.astype(BF16), preferred_element_type=F32) + bglu_ref[...]
    y2 = yg * jax.nn.sigmoid(gate)
    ms = jnp.mean(y2 * y2, axis=0, keepdims=True)
    ns = (y2 * lax.rsqrt(ms + RMS_EPS) * ogs_ref[...]).astype(BF16)
    o = (lax.dot_general(ns, wot_ref[...], (((0,), (0,)), ((), ())), preferred_element_type=F32)
         + jnp.dot(nc_ref[0], wob_ref[...], preferred_element_type=F32))
    x1 = x_ref[0] + mod_ref[0, 2:3, :] * o
    x1_ref[0] = x1
    h2 = _rms(x1) * g2_ref[...] * (1.0 + mod_ref[0, 4:5, :]) + mod_ref[0, 3:4, :]
    rpt = h2.shape[1] // LANES
    for j in range(rpt):
        h2_ref[pl.ds(j, h2.shape[0], stride=rpt), :] = h2[:, j * LANES:(j + 1) * LANES]
    nt_dims = (((1,), (1,)), ((), ()))
    h2_hi = h2.astype(BF16)
    h2_lo = (h2 - h2_hi.astype(F32)).astype(BF16)
    l_hi = lax.dot_general(rw_ref[...], h2_hi, nt_dims, preferred_element_type=F32)
    l_lo = lax.dot_general(rw_ref[:N_EXPERTS, :], h2_lo, nt_dims, preferred_element_type=F32)
    logits = l_hi[:N_EXPERTS, :] + (l_hi[N_EXPERTS:, :] + l_lo) + rb_ref[...]
    expert = lax.broadcasted_iota(jnp.int32, logits.shape, 0)
    vals, idxs = [], []
    for _ in range(TOP_K):
        m = jnp.max(logits, axis=0, keepdims=True)
        sel = jnp.min(jnp.where(logits == m, expert, N_EXPERTS), axis=0, keepdims=True)
        vals.append(m)
        idxs.append(sel)
        logits = jnp.where(expert == sel, -jnp.inf, logits)
    es = [jnp.exp(v - vals[0]) for v in vals]
    tot = es[0] + es[1] + es[2] + es[3]
    idx_ref[...] = jnp.concatenate(idxs, axis=0)
    tw_rows = jnp.concatenate([e / tot for e in es] + [jnp.zeros((LANES - TOP_K, h2.shape[0]), F32)], axis=0)
    tw_ref[...] = tw_rows.T[:, :TOP_K]


def _mix(x, yt, nconv, mod3, wglu_t, bglu, og_ssm, wo_top, wo_bot, norm2_g, router_w, router_b, *, tm):
    bsz, seq, d = x.shape
    nt = seq // tm
    n_tok = bsz * seq
    full = lambda shape: pl.BlockSpec(shape, lambda b, i: (0,) * len(shape))
    return pl.pallas_call(
        _mix_kernel,
        grid=(bsz, nt),
        in_specs=[pl.BlockSpec((1, tm, d), lambda b, i: (b, i, 0)),
                  pl.BlockSpec((D_SSM, tm // CHUNK, CHUNK), lambda b, i: (0, b * nt + i, 0)),
                  pl.BlockSpec((1, tm, D_CONV), lambda b, i: (b, i, 0)),
                  pl.BlockSpec((1, 6, d), lambda b, i: (b, 0, 0)),
                  full((D_SSM, D_SSM)), full((D_SSM, 1)), full((D_SSM, 1)),
                  full((D_SSM, d)), full((D_CONV, d)), full((1, d)),
                  full((2 * N_EXPERTS, d)), full((N_EXPERTS, 1))],
        out_specs=[pl.BlockSpec((1, tm, d), lambda b, i: (b, i, 0)),
                   pl.BlockSpec((tm * d // LANES, LANES), lambda b, i: (b * nt + i, 0)),
                   pl.BlockSpec((TOP_K, tm), lambda b, i: (0, b * nt + i)),
                   pl.BlockSpec((tm, TOP_K), lambda b, i: (b * nt + i, 0))],
        out_shape=[jax.ShapeDtypeStruct((bsz, seq, d), F32),
                   jax.ShapeDtypeStruct((n_tok * d // LANES, LANES), F32),
                   jax.ShapeDtypeStruct((TOP_K, n_tok), jnp.int32),
                   jax.ShapeDtypeStruct((n_tok, TOP_K), F32)],
        compiler_params=pltpu.CompilerParams(dimension_semantics=("arbitrary", "arbitrary"),
                                             vmem_limit_bytes=VMEM_LIMIT),
        name="mix",
    )(x, yt, nconv, mod3, wglu_t, bglu, og_ssm, wo_top, wo_bot, norm2_g, router_w, router_b)


def _moe_kernel(be_ref, nv_ref, inv_ref, h_ref, wg_ref, bg_ref, wu_ref, bu_ref, wd_ref, bd_ref,
                ys_ref, xbuf, ybuf, slots, wg_s, wu_s, wd_s, gsem, ssem, isem,
                *, tb, fc, n_blocks, n_tok):
    i = pl.program_id(0)
    b = i % 2
    nv = nv_ref[i]
    f = wg_s.shape[1]
    d = f
    rpt = d // LANES

    def ring(blk):
        return lax.rem(blk, SLOT_RING) * tb

    def gather_row(buf, base, r):
        tok = slots[base + r] & (n_tok - 1)
        return pltpu.make_async_copy(h_ref.at[pl.ds(pl.multiple_of(tok * rpt, rpt), rpt), :],
                                     xbuf.at[buf, pl.ds(pl.multiple_of(r * rpt, rpt), rpt), :], gsem.at[buf])

    def scatter_row(buf, base, r):
        return pltpu.make_async_copy(ybuf.at[buf, pl.ds(pl.multiple_of(r * rpt, rpt), rpt), :],
                                     ys_ref.at[pl.ds(pl.multiple_of(slots[base + r] * rpt, rpt), rpt), :],
                                     ssem.at[buf])

    def gather_all(buf):
        return pltpu.make_async_copy(h_ref.at[pl.ds(0, tb * rpt), :], xbuf.at[buf], gsem.at[buf])

    def scatter_all(buf):
        return pltpu.make_async_copy(ybuf.at[buf], ys_ref.at[pl.ds(0, tb * rpt), :], ssem.at[buf])

    def gather_one(buf):
        return pltpu.make_async_copy(h_ref.at[pl.ds(0, rpt), :], xbuf.at[buf, pl.ds(0, rpt), :], gsem.at[buf])

    def scatter_one(buf):
        return pltpu.make_async_copy(ybuf.at[buf, pl.ds(0, rpt), :], ys_ref.at[pl.ds(0, rpt), :], ssem.at[buf])

    def on_buffer(which, fn):
        for buf in (0, 1):
            @pl.when(which == buf)
            def _():
                fn(buf)

    def start_rows(n, row_copy, priority=0):
        @pl.when(n == tb)
        def _():
            for r in range(tb):
                row_copy(r).start(priority=priority)

        @pl.when(n < tb)
        def _():
            def body(r, carry):
                row_copy(r).start(priority=priority)
                return carry
            lax.fori_loop(0, n, body, 0)

    def wait_rows(n, all_copy, row_copy):
        @pl.when(n == tb)
        def _():
            all_copy.wait()

        @pl.when(n < tb)
        def _():
            def body(r, carry):
                row_copy.wait()
                return carry
            lax.fori_loop(0, n, body, 0)

    def slots_copy(blk):
        which = lax.rem(blk, SLOT_RING)
        return pltpu.make_async_copy(inv_ref.at[jnp.minimum(blk, n_blocks - 1), 0],
                                     slots.at[pl.ds(which * tb, tb)], isem.at[which])

    def nv_of(blk):
        return jnp.where((blk >= 0) & (blk < n_blocks), nv_ref[jnp.clip(blk, 0, n_blocks - 1)], 0)

    nv_next = nv_of(i + 1)

    @pl.when(i == 0)
    def _():
        xbuf[...] = jnp.zeros_like(xbuf)
        slots_copy(0).start()
        slots_copy(0).wait()
        start_rows(nv, lambda r: gather_row(0, 0, r), priority=ROW_DMA_PRIORITY)

        @pl.when(nv_next > 0)
        def _():
            slots_copy(1).start()

    def wait_inputs(buf):
        wait_rows(nv, gather_all(buf), gather_one(buf))
        wait_rows(nv_of(i - 2), scatter_all(buf), scatter_one(buf))

    on_buffer(b, wait_inputs)

    @pl.when(nv_of(i + 2) > 0)
    def _():
        slots_copy(i + 2).start()

    def prefetch_next(buf):
        slots_copy(i + 1).wait()
        base = ring(i + 1)
        start_rows(nv_next, lambda r: gather_row(buf, base, r), priority=ROW_DMA_PRIORITY)

    @pl.when(nv_next > 0)
    def _():
        on_buffer(1 - b, prefetch_next)

    @pl.when(nv > 0)
    def _():
        @pl.when((i == 0) | (be_ref[i] != be_ref[jnp.maximum(i - 1, 0)]))
        def _():
            wg_s[...] = wg_ref[0].astype(BF16)
            wu_s[...] = wu_ref[0].astype(BF16)
            wd_s[...] = wd_ref[0].astype(BF16)

        xb = jnp.concatenate([xbuf[b, pl.ds(j, tb, stride=rpt), :] for j in range(rpt)], axis=1).astype(BF16)
        y = None
        for j in range(f // fc):
            cols = slice(j * fc, (j + 1) * fc)
            gate = jnp.minimum(jnp.dot(xb, wg_s[:, cols], preferred_element_type=F32) + bg_ref[0, :, cols],
                               SWIGLU_LIMIT)
            up = jnp.clip(jnp.dot(xb, wu_s[:, cols], preferred_element_type=F32) + bu_ref[0, :, cols],
                          -SWIGLU_LIMIT, SWIGLU_LIMIT)
            act = ((up + 1.0) * (gate * jax.nn.sigmoid(SWIGLU_ALPHA * gate))).astype(BF16)
            part = jnp.dot(act, wd_s[cols, :], preferred_element_type=F32)
            y = part if y is None else y + part
        y = y + bd_ref[0]
        for j in range(rpt):
            ybuf[b, pl.ds(j, tb, stride=rpt), :] = y[:, j * LANES:(j + 1) * LANES]

    def send_outputs(buf):
        base = ring(i)
        start_rows(nv, lambda r: scatter_row(buf, base, r), priority=ROW_DMA_PRIORITY)

    on_buffer(b, send_outputs)

    @pl.when(i == n_blocks - 1)
    def _():
        on_buffer(1 - b, lambda buf: wait_rows(nv_of(i - 1), scatter_all(buf), scatter_one(buf)))
        on_buffer(b, lambda buf: wait_rows(nv, scatter_all(buf), scatter_one(buf)))


def _moe(block_e, block_nv, inv, h2t, w_gate, b_gate, w_up, b_up, w_down, b_down, *, tb, fc):
    e, d, f = w_gate.shape
    rpt = d // LANES
    n_tok = h2t.shape[0] // rpt
    n_blocks = block_nv.shape[0]
    assert n_tok & (n_tok - 1) == 0, "token count must be a power of two (slot -> token by masking)"
    assert d == f, "expert width equals model width in this block"
    wspec = lambda shape: pl.BlockSpec((1,) + shape, lambda i, be, nv: (be[i], 0, 0))
    return pl.pallas_call(
        functools.partial(_moe_kernel, tb=tb, fc=fc, n_blocks=n_blocks, n_tok=n_tok),
        grid_spec=pltpu.PrefetchScalarGridSpec(
            num_scalar_prefetch=2,
            grid=(n_blocks,),
            in_specs=[pl.BlockSpec(memory_space=pl.ANY), pl.BlockSpec(memory_space=pl.ANY),
                      wspec((d, f)), wspec((1, f)), wspec((d, f)), wspec((1, f)),
                      wspec((f, d)), wspec((1, d))],
            out_specs=pl.BlockSpec(memory_space=pl.ANY),
            scratch_shapes=[pltpu.VMEM((2, tb * rpt, LANES), F32), pltpu.VMEM((2, tb * rpt, LANES), F32),
                            pltpu.SMEM((SLOT_RING * tb,), jnp.int32),
                            pltpu.VMEM((d, f), BF16), pltpu.VMEM((d, f), BF16), pltpu.VMEM((f, d), BF16),
                            pltpu.SemaphoreType.DMA((2,)), pltpu.SemaphoreType.DMA((2,)),
                            pltpu.SemaphoreType.DMA((SLOT_RING,))]),
        out_shape=jax.ShapeDtypeStruct((TOP_K * n_tok * rpt, LANES), F32),
        compiler_params=pltpu.CompilerParams(dimension_semantics=("arbitrary",),
                                             vmem_limit_bytes=VMEM_LIMIT),
        name="moe",
    )(block_e, block_nv, inv, h2t, w_gate, b_gate.reshape(e, 1, f), w_up, b_up.reshape(e, 1, f),
      w_down, b_down.reshape(e, 1, d))


def _combine_kernel(ys_ref, x1_ref, tw_ref, mod_ref, fg_ref, o_ref):
    tw = tw_ref[...]
    tc, d = x1_ref.shape
    rpt = d // LANES

    def expert_rows(k):
        return jnp.concatenate([ys_ref[k, pl.ds(j, tc, stride=rpt), :] for j in range(rpt)], axis=1)

    moe = tw[:, 0:1] * expert_rows(0)
    for k in range(1, TOP_K):
        moe = moe + tw[:, k:k + 1] * expert_rows(k)
    xo = x1_ref[...] + mod_ref[0, 5:6, :] * moe
    o_ref[...] = _rms(xo) * fg_ref[...]


def _combine(ys4, x1, top_w, mod3, final_g, *, tc, seq):
    n_tok, d = x1.shape
    tiles_per_batch = seq // tc
    return pl.pallas_call(
        _combine_kernel,
        grid=(n_tok // tc,),
        in_specs=[pl.BlockSpec((TOP_K, tc * d // LANES, LANES), lambda i: (0, i, 0)),
                  pl.BlockSpec((tc, d), lambda i: (i, 0)),
                  pl.BlockSpec((tc, TOP_K), lambda i: (i, 0)),
                  pl.BlockSpec((1, 6, d), lambda i: (i // tiles_per_batch, 0, 0)),
                  pl.BlockSpec((1, d), lambda i: (0, 0))],
        out_specs=pl.BlockSpec((tc, d), lambda i: (i, 0)),
        out_shape=jax.ShapeDtypeStruct((n_tok, d), F32),
        compiler_params=pltpu.CompilerParams(dimension_semantics=("arbitrary",),
                                             vmem_limit_bytes=VMEM_LIMIT),
        name="combine",
    )(ys4, x1, top_w, mod3, final_g)


def _forward(x, c, ada_w, ada_b, norm1_g, w_in, lam_re, lam_im, log_dt, b_re, b_im, c_re, c_im, d_skip,
             w_glu, b_glu, conv_w, conv_b, ln_g, ln_b, out_norm_g, w_out, norm2_g, router_w, router_b,
             w_gate, b_gate, w_up, b_up, w_down, b_down, final_g, *, tm, rc, tb, fc, tc):
    bsz, seq, d = x.shape
    n_tok = bsz * seq
    cpb = seq // CHUNK
    n_chunks = n_tok // CHUNK
    row = lambda v: v.reshape(1, -1).astype(F32)

    mod3 = _mod(c, ada_w, ada_b).reshape(bsz, 6, d)

    wu_t = w_in[:, :D_SSM].T.astype(BF16)
    wvg = w_in[:, D_SSM:].astype(BF16)
    cw = jnp.zeros((CONV_HALO, D_CONV), F32).at[:CONV_WIDTH].set(conv_w.astype(F32))
    ut, nconv = _inproj(x, mod3, row(norm1_g), wu_t, wvg, cw, row(conv_b), row(ln_g), row(ln_b),
                        row(out_norm_g[D_SSM:]), tm=tm, rc=rc)

    klag, wst, vout, pq = _ssm_params(lam_re, lam_im, log_dt, b_re, b_im, c_re, c_im, cpb)
    dsk = jnp.broadcast_to(d_skip.astype(F32).reshape(N_GROUPS, SSM_GROUP, 1, 1),
                           (N_GROUPS, SSM_GROUP, 1, CHUNK))
    y4 = _ssm(ut.reshape(N_GROUPS, SSM_GROUP, n_chunks, CHUNK), klag, wst, vout, pq, dsk, cpb=cpb)
    yt = y4.reshape(D_SSM, n_chunks, CHUNK)

    rw = router_w.astype(F32)
    rw_hi = rw.astype(BF16)
    rw_split = jnp.concatenate([rw_hi, (rw - rw_hi.astype(F32)).astype(BF16)], axis=1).T
    x1, h2, top_idx, top_w = _mix(
        x, yt, nconv, mod3, w_glu.T.astype(BF16), b_glu.reshape(D_SSM, 1).astype(F32),
        out_norm_g[:D_SSM].reshape(D_SSM, 1).astype(F32), w_out[:D_SSM].astype(BF16),
        w_out[D_SSM:].astype(BF16), row(norm2_g), rw_split, router_b.reshape(N_EXPERTS, 1).astype(F32), tm=tm)

    n_assign = n_tok * TOP_K
    n_blocks = n_assign // tb + N_EXPERTS
    assert n_assign & (n_assign - 1) == 0 and (N_EXPERTS + 1) * n_assign < 2 ** 31
    experts = jnp.arange(N_EXPERTS, dtype=jnp.int32)
    counts = jnp.sum((top_idx[None] == experts[:, None, None]).astype(jnp.int32), axis=(1, 2))
    padded = (counts + tb - 1) // tb * tb
    pend = jnp.cumsum(padded)
    pstart = pend - padded
    blk_start = jnp.arange(n_blocks, dtype=jnp.int32) * tb
    block_e = jnp.minimum(jnp.sum((blk_start[:, None] >= pend[None, :]).astype(jnp.int32), axis=1), N_EXPERTS - 1)
    block_nv = jnp.clip(pstart[block_e] + counts[block_e] - blk_start, 0, tb).astype(jnp.int32)
    block_nv = jnp.where(blk_start < pend[-1], block_nv, 0)

    slot = jnp.arange(n_assign, dtype=jnp.int32).reshape(TOP_K, n_tok)
    pad_end = jnp.cumsum(padded - counts)
    pad_id = jnp.arange(n_blocks * tb - n_assign, dtype=jnp.int32)
    pad_e = jnp.sum((pad_id[:, None] >= pad_end[None, :]).astype(jnp.int32), axis=1)
    keys = jnp.concatenate([(top_idx * n_assign + slot).reshape(-1), pad_e * n_assign + (n_assign - 1)])
    inv = (lax.sort(keys) & (n_assign - 1)).reshape(n_blocks, 1, tb)

    ys4 = _moe(block_e, block_nv, inv, h2, w_gate, b_gate, w_up, b_up, w_down, b_down, tb=tb, fc=fc)
    out = _combine(ys4.reshape(TOP_K, n_tok * d // LANES, LANES), x1.reshape(n_tok, d), top_w, mod3,
                   row(final_g), tc=tc, seq=seq)
    return out.reshape(bsz, seq, d)


def kernel(x, c, ada_w, ada_b, norm1_g, w_in, lam_re, lam_im, log_dt, b_re, b_im, c_re, c_im, d_skip, w_glu, b_glu, conv_w, conv_b, ln_g, ln_b, out_norm_g, w_out, norm2_g, router_w, router_b, w_gate, b_gate, w_up, b_up, w_down, b_down, final_g):
    p = [a[0] for a in (ada_w, ada_b, norm1_g, w_in, lam_re, lam_im, log_dt, b_re, b_im, c_re, c_im, d_skip,
                        w_glu, b_glu, conv_w, conv_b, ln_g, ln_b, out_norm_g, w_out, norm2_g, router_w,
                        router_b, w_gate, b_gate, w_up, b_up, w_down, b_down)]
    return _forward(x, c, *p, final_g, tm=1024, rc=64, tb=512, fc=256, tc=512)
```

```python
import functools
import math

import jax
import jax.numpy as jnp
from jax import lax
from jax.experimental import pallas as pl
from jax.experimental.pallas import tpu as pltpu

F32 = jnp.float32
BF16 = jnp.bfloat16
HIGHEST = lax.Precision.HIGHEST

D_MODEL = 1024
D_SSM = 512
D_CONV = 512
SSM_GROUP = 16
N_GROUPS = 32
SSM_STATE = 64
CONV_WIDTH = 31
N_EXPERTS = 32
TOP_K = 4
SWIGLU_ALPHA = 1.702
SWIGLU_LIMIT = 7.0
RMS_EPS = 1e-6
LN_EPS = 1e-5

CHUNK = 128
CONV_HALO = 32
SUBLANES = 8
LANES = 128
ROW_DMA_PRIORITY = 1
SLOT_RING = 3
VMEM_LIMIT = 56 * 1024 * 1024


def _tiles():
    return dict(tm=SUBLANES * CHUNK, rc=64, tb=512, fc=512, tc=512)


def _rms(x, eps=RMS_EPS):
    return x * lax.rsqrt(jnp.mean(x * x, axis=-1, keepdims=True) + eps)


def _mod_kernel(c_ref, w_ref, b_ref, o_ref):
    c = c_ref[...]
    cond = c * jax.nn.sigmoid(c)
    o_ref[...] = jnp.dot(cond, w_ref[...], precision=HIGHEST, preferred_element_type=F32) + b_ref[...]


def _mod(c, ada_w, ada_b):
    bsz, d = c.shape
    n = ada_w.shape[1]
    return pl.pallas_call(
        _mod_kernel,
        grid=(n // d,),
        in_specs=[pl.BlockSpec((bsz, d), lambda j: (0, 0)),
                  pl.BlockSpec((d, d), lambda j: (0, j)),
                  pl.BlockSpec((1, d), lambda j: (0, j))],
        out_specs=pl.BlockSpec((bsz, d), lambda j: (0, j)),
        out_shape=jax.ShapeDtypeStruct((bsz, n), F32),
        name="mod",
    )(c, ada_w, ada_b.reshape(1, n))


def _inproj_kernel(x_ref, mod_ref, g1_ref, wu_ref, wvg_ref, cw_ref, cb_ref, lg_ref, lb_ref, og_ref,
                   ut_ref, nc_ref, zext, zsh, *, tm, rc):
    i = pl.program_id(1)
    x = x_ref[0]
    y = _rms(x) * g1_ref[...]
    h = (y * (1.0 + mod_ref[0, 1:2, :]) + mod_ref[0, 0:1, :]).astype(BF16)
    ut = lax.dot_general(wu_ref[...], h, (((1,), (1,)), ((), ())), preferred_element_type=F32)
    for cc in range(tm // CHUNK):
        ut_ref[:, cc, :] = ut[:, cc * CHUNK:(cc + 1) * CHUNK]
    vg = jnp.dot(h, wvg_ref[...], preferred_element_type=F32)
    z = vg[:, :D_CONV] * jax.nn.sigmoid(vg[:, D_CONV:])

    @pl.when(i == 0)
    def _():
        zext[0:CONV_HALO, :] = jnp.zeros((CONV_HALO, D_CONV), F32)

    zext[CONV_HALO:CONV_HALO + tm, :] = z
    off = CONV_HALO - (CONV_WIDTH - 1)
    span = tm + CONV_HALO - SUBLANES
    for p in range(1, SUBLANES):
        zsh[p - 1, 0:span, :] = zext[p:p + span, :]
    for r0 in range(0, tm, rc):
        acc = jnp.zeros((rc, D_CONV), F32) + cb_ref[...]
        for k in range(CONV_WIDTH):
            p = (off + k) % SUBLANES
            lo = r0 + off + k - p
            tap = zext[lo:lo + rc, :] if p == 0 else zsh[p - 1, lo:lo + rc, :]
            acc = acc + cw_ref[k:k + 1, :] * tap
        mu = jnp.mean(acc, axis=-1, keepdims=True)
        xc = acc - mu
        ln = xc * lax.rsqrt(jnp.mean(xc * xc, axis=-1, keepdims=True) + LN_EPS) * lg_ref[...] + lb_ref[...]
        act = ln * jax.nn.sigmoid(ln)
        nc_ref[0, r0:r0 + rc, :] = (_rms(act) * og_ref[...]).astype(BF16)
    zext[0:CONV_HALO, :] = zext[tm:tm + CONV_HALO, :]


def _inproj(x, mod3, norm1_g, wu_t, wvg, conv_w, conv_b, ln_g, ln_b, og_conv, *, tm, rc):
    bsz, seq, d = x.shape
    nt = seq // tm
    full = lambda shape: pl.BlockSpec(shape, lambda b, i: (0,) * len(shape))
    return pl.pallas_call(
        functools.partial(_inproj_kernel, tm=tm, rc=rc),
        grid=(bsz, nt),
        in_specs=[pl.BlockSpec((1, tm, d), lambda b, i: (b, i, 0)),
                  pl.BlockSpec((1, 6, d), lambda b, i: (b, 0, 0)),
                  full((1, d)), full((D_SSM, d)), full((d, 2 * D_CONV)),
                  full((CONV_HALO, D_CONV)), full((1, D_CONV)), full((1, D_CONV)), full((1, D_CONV)),
                  full((1, D_CONV))],
        out_specs=[pl.BlockSpec((D_SSM, tm // CHUNK, CHUNK), lambda b, i: (0, b * nt + i, 0)),
                   pl.BlockSpec((1, tm, D_CONV), lambda b, i: (b, i, 0))],
        out_shape=[jax.ShapeDtypeStruct((D_SSM, bsz * seq // CHUNK, CHUNK), F32),
                   jax.ShapeDtypeStruct((bsz, seq, D_CONV), BF16)],
        scratch_shapes=[pltpu.VMEM((tm + CONV_HALO, D_CONV), F32),
                        pltpu.VMEM((SUBLANES - 1, tm + CONV_HALO, D_CONV), F32)],
        compiler_params=pltpu.CompilerParams(dimension_semantics=("arbitrary", "arbitrary"),
                                             vmem_limit_bytes=VMEM_LIMIT),
        name="inproj",
    )(x, mod3, norm1_g, wu_t, wvg, conv_w, conv_b, ln_g, ln_b, og_conv)


def _ssm_kernel(u_ref, klag_ref, klag_next_ref, w_ref, v_ref, pq_ref, dsk_ref, y_ref, toep_even, toep_odd, *, cpb):
    s_idx = lax.broadcasted_iota(jnp.int32, (CHUNK, CHUNK), 0)
    j_idx = lax.broadcasted_iota(jnp.int32, (CHUNK, CHUNK), 1)
    causal = j_idx >= s_idx

    def expand(lag_ref, toep):
        for hp in range(SSM_GROUP):
            kv = lag_ref[0, hp]
            blocks = []
            for h in range(SSM_GROUP):
                rows = jnp.broadcast_to(kv[h:h + 1, :], (CHUNK, CHUNK))
                skew = pltpu.roll(rows, 0, 1, stride=1, stride_axis=0)
                blocks.append(jnp.where(causal, skew, 0.0).astype(BF16))
            toep[hp * CHUNK:(hp + 1) * CHUNK, :] = jnp.concatenate(blocks, axis=1)

    g = pl.program_id(0)

    @pl.when(g == 0)
    def _():
        expand(klag_ref, toep_even)

    @pl.when(g % 2 == 0)
    def _():
        expand(klag_next_ref, toep_odd)
        _ssm_group(u_ref, toep_even, w_ref, v_ref, pq_ref, dsk_ref, y_ref, cpb=cpb)

    @pl.when(g % 2 == 1)
    def _():
        expand(klag_next_ref, toep_even)
        _ssm_group(u_ref, toep_odd, w_ref, v_ref, pq_ref, dsk_ref, y_ref, cpb=cpb)


def _ssm_group(u_ref, toep, w_ref, v_ref, pq_ref, dsk_ref, y_ref, *, cpb):
    nc = u_ref.shape[2]
    us = [u_ref[0, h] for h in range(SSM_GROUP)]
    xcat = jnp.concatenate([u.astype(BF16) for u in us], axis=1)
    acc = jnp.dot(xcat, toep[...], preferred_element_type=F32)
    st = jnp.dot(xcat, w_ref[0], preferred_element_type=F32)
    row = lax.broadcasted_iota(jnp.int32, (nc, 2 * SSM_STATE), 0) % cpb
    d = 1
    step = 0
    while d < cpb:
        sh = jnp.where(row >= d, pltpu.roll(st, d, axis=0), 0.0)
        st = st + pq_ref[0, step, 0:1, :] * sh + pq_ref[0, step, 1:2, :] * pltpu.roll(sh, SSM_STATE, axis=1)
        d *= 2
        step += 1
    prev = jnp.where(row >= 1, pltpu.roll(st, 1, axis=0), 0.0)
    acc = acc + jnp.dot(prev.astype(BF16), v_ref[0], preferred_element_type=F32)
    for h in range(SSM_GROUP):
        y_ref[0, h] = acc[:, h * CHUNK:(h + 1) * CHUNK] + dsk_ref[0, h] * us[h]


def _ssm(u4, klag, wst, vout, pq, dsk, *, cpb):
    g, hh, nc, t = u4.shape
    blk = lambda shape: pl.BlockSpec((1,) + shape, lambda i: (i,) + (0,) * len(shape))
    return pl.pallas_call(
        functools.partial(_ssm_kernel, cpb=cpb),
        grid=(g,),
        in_specs=[blk((hh, nc, t)), blk(klag.shape[1:]),
                  pl.BlockSpec((1,) + klag.shape[1:], lambda i: (jnp.minimum(i + 1, g - 1), 0, 0, 0)),
                  blk(wst.shape[1:]), blk(vout.shape[1:]), blk(pq.shape[1:]), blk(dsk.shape[1:])],
        out_specs=blk((hh, nc, t)),
        out_shape=jax.ShapeDtypeStruct(u4.shape, F32),
        scratch_shapes=[pltpu.VMEM((hh * t, hh * t), BF16), pltpu.VMEM((hh * t, hh * t), BF16)],
        compiler_params=pltpu.CompilerParams(dimension_semantics=("arbitrary",),
                                             vmem_limit_bytes=VMEM_LIMIT),
        name="ssm",
    )(u4, klag, klag, wst, vout, pq, dsk)


def _ssm_params(lam_re, lam_im, log_dt, b_re, b_im, c_re, c_im, cpb):
    lr, li = lam_re.astype(F32), lam_im.astype(F32)
    dt = jnp.exp(log_dt.astype(F32))[:, None]
    mag = jnp.exp(lr * dt)
    ab_re = mag * jnp.cos(li * dt)
    ab_im = mag * jnp.sin(li * dt)
    den = lr * lr + li * li
    nr = ab_re - 1.0
    q_re = (nr * lr + ab_im * li) / den
    q_im = (ab_im * lr - nr * li) / den
    br, bi = b_re.astype(F32), b_im.astype(F32)
    bb_re = q_re[..., None] * br - q_im[..., None] * bi
    bb_im = q_re[..., None] * bi + q_im[..., None] * br
    cr, ci = c_re.astype(F32), c_im.astype(F32)

    pr, pi = jnp.ones((1,) + ab_re.shape, F32), jnp.zeros((1,) + ab_re.shape, F32)
    cur_r, cur_i = ab_re, ab_im
    while pr.shape[0] < CHUNK:
        pr, pi = (jnp.concatenate([pr, pr * cur_r - pi * cur_i], axis=0),
                  jnp.concatenate([pi, pr * cur_i + pi * cur_r], axis=0))
        cur_r, cur_i = cur_r * cur_r - cur_i * cur_i, 2.0 * cur_r * cur_i
    cp_re = cr[None] * pr[:, :, None, :] - ci[None] * pi[:, :, None, :]
    cp_im = cr[None] * pi[:, :, None, :] + ci[None] * pr[:, :, None, :]
    klag = (jnp.einsum('lghp,gpk->gkhl', cp_re, bb_re, precision=HIGHEST)
            - jnp.einsum('lghp,gpk->gkhl', cp_im, bb_im, precision=HIGHEST))
    rr, ri = pr[::-1], pi[::-1]
    w_re = rr[..., None] * bb_re[None] - ri[..., None] * bb_im[None]
    w_im = rr[..., None] * bb_im[None] + ri[..., None] * bb_re[None]
    wst = jnp.concatenate([w_re, w_im], axis=2).transpose(1, 3, 0, 2)
    wst = wst.reshape(N_GROUPS, SSM_GROUP * CHUNK, 2 * SSM_STATE).astype(BF16)
    p1r, p1i = pr * ab_re - pi * ab_im, pr * ab_im + pi * ab_re
    v_re = cr[None] * p1r[:, :, None, :] - ci[None] * p1i[:, :, None, :]
    v_im = cr[None] * p1i[:, :, None, :] + ci[None] * p1r[:, :, None, :]
    vout = jnp.concatenate([v_re, -v_im], axis=3).transpose(1, 3, 2, 0)
    vout = vout.reshape(N_GROUPS, 2 * SSM_STATE, SSM_GROUP * CHUNK).astype(BF16)
    tabs = []
    d = 1
    while d < cpb or not tabs:
        tabs.append(jnp.stack([jnp.concatenate([cur_r, cur_r], axis=1),
                               jnp.concatenate([-cur_i, cur_i], axis=1)], axis=1))
        cur_r, cur_i = cur_r * cur_r - cur_i * cur_i, 2.0 * cur_r * cur_i
        d *= 2
    pq = jnp.stack(tabs, axis=1)
    return klag, wst, vout, pq


def _mix_kernel(x_ref, yt_ref, nc_ref, mod_ref, wglu_ref, bglu_ref, ogs_ref, wot_ref, wob_ref, g2_ref,
                rw_ref, rb_ref, x1_ref, h2_ref, idx_ref, tw_ref):
    yt = jnp.concatenate([yt_ref[:, cc, :] for cc in range(yt_ref.shape[1])], axis=1)
    yg = 0.5 * yt * (1.0 + jnp.tanh(math.sqrt(2.0 / math.pi) * (yt + 0.044715 * (yt * yt * yt))))
    gate = jnp.dot(wglu_ref[...], yg.astype(BF16), preferred_element_type=F32) + bglu_ref[...]
    y2 = yg * jax.nn.sigmoid(gate)
    ms = jnp.mean(y2 * y2, axis=0, keepdims=True)
    ns = (y2 * lax.rsqrt(ms + RMS_EPS) * ogs_ref[...]).astype(BF16)
    o = (lax.dot_general(ns, wot_ref[...], (((0,), (0,)), ((), ())), preferred_element_type=F32)
         + jnp.dot(nc_ref[0], wob_ref[...], preferred_element_type=F32))
    x1 = x_ref[0] + mod_ref[0, 2:3, :] * o
    x1_ref[0] = x1
    h2 = _rms(x1) * g2_ref[...] * (1.0 + mod_ref[0, 4:5, :]) + mod_ref[0, 3:4, :]
    rpt = h2.shape[1] // LANES
    for j in range(rpt):
        h2_ref[pl.ds(j, h2.shape[0], stride=rpt), :] = h2[:, j * LANES:(j + 1) * LANES]
    nt_dims = (((1,), (1,)), ((), ()))
    h2_hi = h2.astype(BF16)
    h2_lo = (h2 - h2_hi.astype(F32)).astype(BF16)
    l_hi = lax.dot_general(rw_ref[...], h2_hi, nt_dims, preferred_element_type=F32)
    l_lo = lax.dot_general(rw_ref[:N_EXPERTS, :], h2_lo, nt_dims, preferred_element_type=F32)
    logits = l_hi[:N_EXPERTS, :] + (l_hi[N_EXPERTS:, :] + l_lo) + rb_ref[...]
    expert = lax.broadcasted_iota(jnp.int32, logits.shape, 0)
    vals, idxs = [], []
    for _ in range(TOP_K):
        m = jnp.max(logits, axis=0, keepdims=True)
        sel = jnp.min(jnp.where(logits == m, expert, N_EXPERTS), axis=0, keepdims=True)
        vals.append(m)
        idxs.append(sel)
        logits = jnp.where(expert == sel, -jnp.inf, logits)
    es = [jnp.exp(v - vals[0]) for v in vals]
    tot = es[0] + es[1] + es[2] + es[3]
    idx_ref[...] = jnp.concatenate(idxs, axis=0)
    tw_rows = jnp.concatenate([e / tot for e in es] + [jnp.zeros((LANES - TOP_K, h2.shape[0]), F32)], axis=0)
    tw_ref[...] = tw_rows.T[:, :TOP_K]


def _mix(x, yt, nconv, mod3, wglu_t, bglu, og_ssm, wo_top, wo_bot, norm2_g, router_w, router_b, *, tm):
    bsz, seq, d = x.shape
    nt = seq // tm
    n_tok = bsz * seq
    full = lambda shape: pl.BlockSpec(shape, lambda b, i: (0,) * len(shape))
    return pl.pallas_call(
        _mix_kernel,
        grid=(bsz, nt),
        in_specs=[pl.BlockSpec((1, tm, d), lambda b, i: (b, i, 0)),
                  pl.BlockSpec((D_SSM, tm // CHUNK, CHUNK), lambda b, i: (0, b * nt + i, 0)),
                  pl.BlockSpec((1, tm, D_CONV), lambda b, i: (b, i, 0)),
                  pl.BlockSpec((1, 6, d), lambda b, i: (b, 0, 0)),
                  full((D_SSM, D_SSM)), full((D_SSM, 1)), full((D_SSM, 1)),
                  full((D_SSM, d)), full((D_CONV, d)), full((1, d)),
                  full((2 * N_EXPERTS, d)), full((N_EXPERTS, 1))],
        out_specs=[pl.BlockSpec((1, tm, d), lambda b, i: (b, i, 0)),
                   pl.BlockSpec((tm * d // LANES, LANES), lambda b, i: (b * nt + i, 0)),
                   pl.BlockSpec((TOP_K, tm), lambda b, i: (0, b * nt + i)),
                   pl.BlockSpec((tm, TOP_K), lambda b, i: (b * nt + i, 0))],
        out_shape=[jax.ShapeDtypeStruct((bsz, seq, d), F32),
                   jax.ShapeDtypeStruct((n_tok * d // LANES, LANES), F32),
                   jax.ShapeDtypeStruct((TOP_K, n_tok), jnp.int32),
                   jax.ShapeDtypeStruct((n_tok, TOP_K), F32)],
        compiler_params=pltpu.CompilerParams(dimension_semantics=("arbitrary", "arbitrary"),
                                             vmem_limit_bytes=VMEM_LIMIT),
        name="mix",
    )(x, yt, nconv, mod3, wglu_t, bglu, og_ssm, wo_top, wo_bot, norm2_g, router_w, router_b)


def _moe_kernel(be_ref, nv_ref, inv_ref, h_ref, wg_ref, bg_ref, wu_ref, bu_ref, wd_ref, bd_ref,
                ys_ref, xbuf, ybuf, slots, wg_s, wu_s, wd_s, gsem, ssem, isem,
                *, tb, fc, n_blocks, n_tok):
    i = pl.program_id(0)
    b = i % 2
    nv = nv_ref[i]
    f = wg_s.shape[1]
    d = f
    rpt = d // LANES

    def ring(blk):
        return lax.rem(blk, SLOT_RING) * tb

    def gather_row(buf, base, r):
        tok = slots[base + r] & (n_tok - 1)
        return pltpu.make_async_copy(h_ref.at[pl.ds(pl.multiple_of(tok * rpt, rpt), rpt), :],
                                     xbuf.at[buf, pl.ds(pl.multiple_of(r * rpt, rpt), rpt), :], gsem.at[buf])

    def scatter_row(buf, base, r):
        return pltpu.make_async_copy(ybuf.at[buf, pl.ds(pl.multiple_of(r * rpt, rpt), rpt), :],
                                     ys_ref.at[pl.ds(pl.multiple_of(slots[base + r] * rpt, rpt), rpt), :],
                                     ssem.at[buf])

    def gather_all(buf):
        return pltpu.make_async_copy(h_ref.at[pl.ds(0, tb * rpt), :], xbuf.at[buf], gsem.at[buf])

    def scatter_all(buf):
        return pltpu.make_async_copy(ybuf.at[buf], ys_ref.at[pl.ds(0, tb * rpt), :], ssem.at[buf])

    def gather_one(buf):
        return pltpu.make_async_copy(h_ref.at[pl.ds(0, rpt), :], xbuf.at[buf, pl.ds(0, rpt), :], gsem.at[buf])

    def scatter_one(buf):
        return pltpu.make_async_copy(ybuf.at[buf, pl.ds(0, rpt), :], ys_ref.at[pl.ds(0, rpt), :], ssem.at[buf])

    def on_buffer(which, fn):
        for buf in (0, 1):
            @pl.when(which == buf)
            def _():
                fn(buf)

    def start_rows(n, row_copy, priority=0):
        @pl.when(n == tb)
        def _():
            for r in range(tb):
                row_copy(r).start(priority=priority)

        @pl.when(n < tb)
        def _():
            def body(r, carry):
                row_copy(r).start(priority=priority)
                return carry
            lax.fori_loop(0, n, body, 0)

    def wait_rows(n, all_copy, row_copy):
        @pl.when(n == tb)
        def _():
            all_copy.wait()

        @pl.when(n < tb)
        def _():
            def body(r, carry):
                row_copy.wait()
                return carry
            lax.fori_loop(0, n, body, 0)

    def slots_copy(blk):
        which = lax.rem(blk, SLOT_RING)
        return pltpu.make_async_copy(inv_ref.at[jnp.minimum(blk, n_blocks - 1), 0],
                                     slots.at[pl.ds(which * tb, tb)], isem.at[which])

    def nv_of(blk):
        return jnp.where((blk >= 0) & (blk < n_blocks), nv_ref[jnp.clip(blk, 0, n_blocks - 1)], 0)

    nv_next = nv_of(i + 1)

    @pl.when(i == 0)
    def _():
        xbuf[...] = jnp.zeros_like(xbuf)
        slots_copy(0).start()
        slots_copy(0).wait()
        start_rows(nv, lambda r: gather_row(0, 0, r), priority=ROW_DMA_PRIORITY)

        @pl.when(nv_next > 0)
        def _():
            slots_copy(1).start()

    def wait_inputs(buf):
        wait_rows(nv, gather_all(buf), gather_one(buf))
        wait_rows(nv_of(i - 2), scatter_all(buf), scatter_one(buf))

    on_buffer(b, wait_inputs)

    @pl.when(nv_of(i + 2) > 0)
    def _():
        slots_copy(i + 2).start()

    def prefetch_next(buf):
        slots_copy(i + 1).wait()
        base = ring(i + 1)
        start_rows(nv_next, lambda r: gather_row(buf, base, r), priority=ROW_DMA_PRIORITY)

    @pl.when(nv_next > 0)
    def _():
        on_buffer(1 - b, prefetch_next)

    @pl.when(nv > 0)
    def _():
        @pl.when((i == 0) | (be_ref[i] != be_ref[jnp.maximum(i - 1, 0)]))
        def _():
            wg_s[...] = wg_ref[0].astype(BF16)
            wu_s[...] = wu_ref[0].astype(BF16)
            wd_s[...] = wd_ref[0].astype(BF16)

        xb = jnp.concatenate([xbuf[b, pl.ds(j, tb, stride=rpt), :] for j in range(rpt)], axis=1).astype(BF16)
        y = None
        for j in range(f // fc):
            cols = slice(j * fc, (j + 1) * fc)
            gate = jnp.minimum(jnp.dot(xb, wg_s[:, cols], preferred_element_type=F32) + bg_ref[0, :, cols],
                               SWIGLU_LIMIT)
            up = jnp.clip(jnp.dot(xb, wu_s[:, cols], preferred_element_type=F32) + bu_ref[0, :, cols],
                          -SWIGLU_LIMIT, SWIGLU_LIMIT)
            act = ((up + 1.0) * (gate * jax.nn.sigmoid(SWIGLU_ALPHA * gate))).astype(BF16)
            part = jnp.dot(act, wd_s[cols, :], preferred_element_type=F32)
            y = part if y is None else y + part
        y = y + bd_ref[0]
        for j in range(rpt):
            ybuf[b, pl.ds(j, tb, stride=rpt), :] = y[:, j * LANES:(j + 1) * LANES]

    def send_outputs(buf):
        base = ring(i)
        start_rows(nv, lambda r: scatter_row(buf, base, r), priority=ROW_DMA_PRIORITY)

    on_buffer(b, send_outputs)

    @pl.when(i == n_blocks - 1)
    def _():
        on_buffer(1 - b, lambda buf: wait_rows(nv_of(i - 1), scatter_all(buf), scatter_one(buf)))
        on_buffer(b, lambda buf: wait_rows(nv, scatter_all(buf), scatter_one(buf)))


def _moe(block_e, block_nv, inv, h2t, w_gate, b_gate, w_up, b_up, w_down, b_down, *, tb, fc):
    e, d, f = w_gate.shape
    rpt = d // LANES
    n_tok = h2t.shape[0] // rpt
    n_blocks = block_nv.shape[0]
    assert n_tok & (n_tok - 1) == 0, "token count must be a power of two (slot -> token by masking)"
    assert d == f, "expert width equals model width in this block"
    wspec = lambda shape: pl.BlockSpec((1,) + shape, lambda i, be, nv: (be[i], 0, 0))
    return pl.pallas_call(
        functools.partial(_moe_kernel, tb=tb, fc=fc, n_blocks=n_blocks, n_tok=n_tok),
        grid_spec=pltpu.PrefetchScalarGridSpec(
            num_scalar_prefetch=2,
            grid=(n_blocks,),
            in_specs=[pl.BlockSpec(memory_space=pl.ANY), pl.BlockSpec(memory_space=pl.ANY),
                      wspec((d, f)), wspec((1, f)), wspec((d, f)), wspec((1, f)),
                      wspec((f, d)), wspec((1, d))],
            out_specs=pl.BlockSpec(memory_space=pl.ANY),
            scratch_shapes=[pltpu.VMEM((2, tb * rpt, LANES), F32), pltpu.VMEM((2, tb * rpt, LANES), F32),
                            pltpu.SMEM((SLOT_RING * tb,), jnp.int32),
                            pltpu.VMEM((d, f), BF16), pltpu.VMEM((d, f), BF16), pltpu.VMEM((f, d), BF16),
                            pltpu.SemaphoreType.DMA((2,)), pltpu.SemaphoreType.DMA((2,)),
                            pltpu.SemaphoreType.DMA((SLOT_RING,))]),
        out_shape=jax.ShapeDtypeStruct((TOP_K * n_tok * rpt, LANES), F32),
        compiler_params=pltpu.CompilerParams(dimension_semantics=("arbitrary",),
                                             vmem_limit_bytes=VMEM_LIMIT),
        name="moe",
    )(block_e, block_nv, inv, h2t, w_gate, b_gate.reshape(e, 1, f), w_up, b_up.reshape(e, 1, f),
      w_down, b_down.reshape(e, 1, d))


def _combine_kernel(ys_ref, x1_ref, tw_ref, mod_ref, fg_ref, o_ref):
    tw = tw_ref[...]
    tc, d = x1_ref.shape
    rpt = d // LANES

    def expert_rows(k):
        return jnp.concatenate([ys_ref[k, pl.ds(j, tc, stride=rpt), :] for j in range(rpt)], axis=1)

    moe = tw[:, 0:1] * expert_rows(0)
    for k in range(1, TOP_K):
        moe = moe + tw[:, k:k + 1] * expert_rows(k)
    xo = x1_ref[...] + mod_ref[0, 5:6, :] * moe
    o_ref[...] = _rms(xo) * fg_ref[...]


def _combine(ys4, x1, top_w, mod3, final_g, *, tc, seq):
    n_tok, d = x1.shape
    tiles_per_batch = seq // tc
    return pl.pallas_call(
        _combine_kernel,
        grid=(n_tok // tc,),
        in_specs=[pl.BlockSpec((TOP_K, tc * d // LANES, LANES), lambda i: (0, i, 0)),
                  pl.BlockSpec((tc, d), lambda i: (i, 0)),
                  pl.BlockSpec((tc, TOP_K), lambda i: (i, 0)),
                  pl.BlockSpec((1, 6, d), lambda i: (i // tiles_per_batch, 0, 0)),
                  pl.BlockSpec((1, d), lambda i: (0, 0))],
        out_specs=pl.BlockSpec((tc, d), lambda i: (i, 0)),
        out_shape=jax.ShapeDtypeStruct((n_tok, d), F32),
        compiler_params=pltpu.CompilerParams(dimension_semantics=("arbitrary",),
                                             vmem_limit_bytes=VMEM_LIMIT),
        name="combine",
    )(ys4, x1, top_w, mod3, final_g)


def _forward(x, c, ada_w, ada_b, norm1_g, w_in, lam_re, lam_im, log_dt, b_re, b_im, c_re, c_im, d_skip,
             w_glu, b_glu, conv_w, conv_b, ln_g, ln_b, out_norm_g, w_out, norm2_g, router_w, router_b,
             w_gate, b_gate, w_up, b_up, w_down, b_down, final_g, *, tm, rc, tb, fc, tc):
    bsz, seq, d = x.shape
    n_tok = bsz * seq
    cpb = seq // CHUNK
    n_chunks = n_tok // CHUNK
    row = lambda v: v.reshape(1, -1).astype(F32)

    mod3 = _mod(c, ada_w, ada_b).reshape(bsz, 6, d)

    wu_t = w_in[:, :D_SSM].T.astype(BF16)
    wvg = w_in[:, D_SSM:].astype(BF16)
    cw = jnp.zeros((CONV_HALO, D_CONV), F32).at[:CONV_WIDTH].set(conv_w.astype(F32))
    ut, nconv = _inproj(x, mod3, row(norm1_g), wu_t, wvg, cw, row(conv_b), row(ln_g), row(ln_b),
                        row(out_norm_g[D_SSM:]), tm=tm, rc=rc)

    klag, wst, vout, pq = _ssm_params(lam_re, lam_im, log_dt, b_re, b_im, c_re, c_im, cpb)
    dsk = jnp.broadcast_to(d_skip.astype(F32).reshape(N_GROUPS, SSM_GROUP, 1, 1),
                           (N_GROUPS, SSM_GROUP, 1, CHUNK))
    y4 = _ssm(ut.reshape(N_GROUPS, SSM_GROUP, n_chunks, CHUNK), klag, wst, vout, pq, dsk, cpb=cpb)
    yt = y4.reshape(D_SSM, n_chunks, CHUNK)

    rw = router_w.astype(F32)
    rw_hi = rw.astype(BF16)
    rw_split = jnp.concatenate([rw_hi, (rw - rw_hi.astype(F32)).astype(BF16)], axis=1).T
    x1, h2, top_idx, top_w = _mix(
        x, yt, nconv, mod3, w_glu.T.astype(BF16), b_glu.reshape(D_SSM, 1).astype(F32),
        out_norm_g[:D_SSM].reshape(D_SSM, 1).astype(F32), w_out[:D_SSM].astype(BF16),
        w_out[D_SSM:].astype(BF16), row(norm2_g), rw_split, router_b.reshape(N_EXPERTS, 1).astype(F32), tm=tm)

    n_assign = n_tok * TOP_K
    n_blocks = n_assign // tb + N_EXPERTS
    assert n_assign & (n_assign - 1) == 0 and (N_EXPERTS + 1) * n_assign < 2 ** 31
    experts = jnp.arange(N_EXPERTS, dtype=jnp.int32)
    counts = jnp.sum((top_idx[None] == experts[:, None, None]).astype(jnp.int32), axis=(1, 2))
    padded = (counts + tb - 1) // tb * tb
    pend = jnp.cumsum(padded)
    pstart = pend - padded
    blk_start = jnp.arange(n_blocks, dtype=jnp.int32) * tb
    block_e = jnp.minimum(jnp.sum((blk_start[:, None] >= pend[None, :]).astype(jnp.int32), axis=1), N_EXPERTS - 1)
    block_nv = jnp.clip(pstart[block_e] + counts[block_e] - blk_start, 0, tb).astype(jnp.int32)
    block_nv = jnp.where(blk_start < pend[-1], block_nv, 0)

    slot = jnp.arange(n_assign, dtype=jnp.int32).reshape(TOP_K, n_tok)
    pad_end = jnp.cumsum(padded - counts)
    pad_id = jnp.arange(n_blocks * tb - n_assign, dtype=jnp.int32)
    pad_e = jnp.sum((pad_id[:, None] >= pad_end[None, :]).astype(jnp.int32), axis=1)
    keys = jnp.concatenate([(top_idx * n_assign + slot).reshape(-1), pad_e * n_assign + (n_assign - 1)])
    inv = (lax.sort(keys) & (n_assign - 1)).reshape(n_blocks, 1, tb)

    ys4 = _moe(block_e, block_nv, inv, h2, w_gate, b_gate, w_up, b_up, w_down, b_down, tb=tb, fc=fc)
    out = _combine(ys4.reshape(TOP_K, n_tok * d // LANES, LANES), x1.reshape(n_tok, d), top_w, mod3,
                   row(final_g), tc=tc, seq=seq)
    return out.reshape(bsz, seq, d)


def kernel(x, c, ada_w, ada_b, norm1_g, w_in, lam_re, lam_im, log_dt, b_re, b_im, c_re, c_im, d_skip, w_glu, b_glu, conv_w, conv_b, ln_g, ln_b, out_norm_g, w_out, norm2_g, router_w, router_b, w_gate, b_gate, w_up, b_up, w_down, b_down, final_g):
    p = [a[0] for a in (ada_w, ada_b, norm1_g, w_in, lam_re, lam_im, log_dt, b_re, b_im, c_re, c_im, d_skip,
                        w_glu, b_glu, conv_w, conv_b, ln_g, ln_b, out_norm_g, w_out, norm2_g, router_w,
                        router_b, w_gate, b_gate, w_up, b_up, w_down, b_down)]
    return _forward(x, c, *p, final_g, **_tiles())
```

```python
import functools
import math

import jax
import jax.numpy as jnp
from jax import lax
from jax.experimental import pallas as pl
from jax.experimental.pallas import tpu as pltpu

F32 = jnp.float32
BF16 = jnp.bfloat16
HIGHEST = lax.Precision.HIGHEST

D_MODEL = 1024
D_SSM = 512
D_CONV = 512
SSM_GROUP = 16
N_GROUPS = 32
SSM_STATE = 64
CONV_WIDTH = 31
N_EXPERTS = 32
TOP_K = 4
SWIGLU_ALPHA = 1.702
SWIGLU_LIMIT = 7.0
RMS_EPS = 1e-6
LN_EPS = 1e-5

CHUNK = 128
CONV_HALO = 32
SUBLANES = 8
LANES = 128
ROW_DMA_PRIORITY = 1
SLOT_RING = 3
VMEM_LIMIT = 56 * 1024 * 1024


def _tiles():
    return dict(tm=SUBLANES * CHUNK, rc=128, tb=512, fc=512, tc=512)


def _rms(x, eps=RMS_EPS):
    return x * lax.rsqrt(jnp.mean(x * x, axis=-1, keepdims=True) + eps)


def _mod_kernel(c_ref, w_ref, b_ref, o_ref):
    c = c_ref[...]
    cond = c * jax.nn.sigmoid(c)
    o_ref[...] = jnp.dot(cond, w_ref[...], precision=HIGHEST, preferred_element_type=F32) + b_ref[...]


def _mod(c, ada_w, ada_b):
    bsz, d = c.shape
    n = ada_w.shape[1]
    return pl.pallas_call(
        _mod_kernel,
        grid=(n // d,),
        in_specs=[pl.BlockSpec((bsz, d), lambda j: (0, 0)),
                  pl.BlockSpec((d, d), lambda j: (0, j)),
                  pl.BlockSpec((1, d), lambda j: (0, j))],
        out_specs=pl.BlockSpec((bsz, d), lambda j: (0, j)),
        out_shape=jax.ShapeDtypeStruct((bsz, n), F32),
        name="mod",
    )(c, ada_w, ada_b.reshape(1, n))


def _inproj_kernel(x_ref, mod_ref, g1_ref, wu_ref, wvg_ref, cw_ref, cb_ref, lg_ref, lb_ref, og_ref,
                   ut_ref, nc_ref, zext, zsh, *, tm, rc):
    i = pl.program_id(1)
    x = x_ref[0]
    y = _rms(x) * g1_ref[...]
    h = (y * (1.0 + mod_ref[0, 1:2, :]) + mod_ref[0, 0:1, :]).astype(BF16)
    ut = lax.dot_general(wu_ref[...], h, (((1,), (1,)), ((), ())), preferred_element_type=F32)
    for cc in range(tm // CHUNK):
        ut_ref[:, cc, :] = ut[:, cc * CHUNK:(cc + 1) * CHUNK]
    vg = jnp.dot(h, wvg_ref[...], preferred_element_type=F32)
    z = vg[:, :D_CONV] * jax.nn.sigmoid(vg[:, D_CONV:])

    @pl.when(i == 0)
    def _():
        zext[0:CONV_HALO, :] = jnp.zeros((CONV_HALO, D_CONV), F32)

    zext[CONV_HALO:CONV_HALO + tm, :] = z
    off = CONV_HALO - (CONV_WIDTH - 1)
    span = tm + CONV_HALO - SUBLANES
    for p in range(1, SUBLANES):
        zsh[p - 1, 0:span, :] = zext[p:p + span, :]
    for r0 in range(0, tm, rc):
        acc = jnp.zeros((rc, D_CONV), F32) + cb_ref[...]
        for k in range(CONV_WIDTH):
            p = (off + k) % SUBLANES
            lo = r0 + off + k - p
            tap = zext[lo:lo + rc, :] if p == 0 else zsh[p - 1, lo:lo + rc, :]
            acc = acc + cw_ref[k:k + 1, :] * tap
        mu = jnp.mean(acc, axis=-1, keepdims=True)
        xc = acc - mu
        ln = xc * lax.rsqrt(jnp.mean(xc * xc, axis=-1, keepdims=True) + LN_EPS) * lg_ref[...] + lb_ref[...]
        act = ln * jax.nn.sigmoid(ln)
        nc_ref[0, r0:r0 + rc, :] = (_rms(act) * og_ref[...]).astype(BF16)
    zext[0:CONV_HALO, :] = zext[tm:tm + CONV_HALO, :]


def _inproj(x, mod3, norm1_g, wu_t, wvg, conv_w, conv_b, ln_g, ln_b, og_conv, *, tm, rc):
    bsz, seq, d = x.shape
    nt = seq // tm
    full = lambda shape: pl.BlockSpec(shape, lambda b, i: (0,) * len(shape))
    return pl.pallas_call(
        functools.partial(_inproj_kernel, tm=tm, rc=rc),
        grid=(bsz, nt),
        in_specs=[pl.BlockSpec((1, tm, d), lambda b, i: (b, i, 0)),
                  pl.BlockSpec((1, 6, d), lambda b, i: (b, 0, 0)),
                  full((1, d)), full((D_SSM, d)), full((d, 2 * D_CONV)),
                  full((CONV_HALO, D_CONV)), full((1, D_CONV)), full((1, D_CONV)), full((1, D_CONV)),
                  full((1, D_CONV))],
        out_specs=[pl.BlockSpec((D_SSM, tm // CHUNK, CHUNK), lambda b, i: (0, b * nt + i, 0)),
                   pl.BlockSpec((1, tm, D_CONV), lambda b, i: (b, i, 0))],
        out_shape=[jax.ShapeDtypeStruct((D_SSM, bsz * seq // CHUNK, CHUNK), F32),
                   jax.ShapeDtypeStruct((bsz, seq, D_CONV), BF16)],
        scratch_shapes=[pltpu.VMEM((tm + CONV_HALO, D_CONV), F32),
                        pltpu.VMEM((SUBLANES - 1, tm + CONV_HALO, D_CONV), F32)],
        compiler_params=pltpu.CompilerParams(dimension_semantics=("arbitrary", "arbitrary"),
                                             vmem_limit_bytes=VMEM_LIMIT),
        name="inproj",
    )(x, mod3, norm1_g, wu_t, wvg, conv_w, conv_b, ln_g, ln_b, og_conv)


def _ssm_kernel(u_ref, klag_ref, klag_next_ref, w_ref, v_ref, pq_ref, dsk_ref, y_ref, toep_even, toep_odd, *, cpb):
    s_idx = lax.broadcasted_iota(jnp.int32, (CHUNK, CHUNK), 0)
    j_idx = lax.broadcasted_iota(jnp.int32, (CHUNK, CHUNK), 1)
    causal = j_idx >= s_idx

    def expand(lag_ref, toep):
        for hp in range(SSM_GROUP):
            kv = lag_ref[0, hp]
            blocks = []
            for h in range(SSM_GROUP):
                rows = jnp.broadcast_to(kv[h:h + 1, :], (CHUNK, CHUNK))
                skew = pltpu.roll(rows, 0, 1, stride=1, stride_axis=0)
                blocks.append(jnp.where(causal, skew, 0.0).astype(BF16))
            toep[hp * CHUNK:(hp + 1) * CHUNK, :] = jnp.concatenate(blocks, axis=1)

    g = pl.program_id(0)

    @pl.when(g == 0)
    def _():
        expand(klag_ref, toep_even)

    @pl.when(g % 2 == 0)
    def _():
        expand(klag_next_ref, toep_odd)
        _ssm_group(u_ref, toep_even, w_ref, v_ref, pq_ref, dsk_ref, y_ref, cpb=cpb)

    @pl.when(g % 2 == 1)
    def _():
        expand(klag_next_ref, toep_even)
        _ssm_group(u_ref, toep_odd, w_ref, v_ref, pq_ref, dsk_ref, y_ref, cpb=cpb)


def _ssm_group(u_ref, toep, w_ref, v_ref, pq_ref, dsk_ref, y_ref, *, cpb):
    nc = u_ref.shape[2]
    us = [u_ref[0, h] for h in range(SSM_GROUP)]
    xcat = jnp.concatenate([u.astype(BF16) for u in us], axis=1)
    acc = jnp.dot(xcat, toep[...], preferred_element_type=F32)
    st = jnp.dot(xcat, w_ref[0], preferred_element_type=F32)
    row = lax.broadcasted_iota(jnp.int32, (nc, 2 * SSM_STATE), 0) % cpb
    d = 1
    step = 0
    while d < cpb:
        sh = jnp.where(row >= d, pltpu.roll(st, d, axis=0), 0.0)
        st = st + pq_ref[0, step, 0:1, :] * sh + pq_ref[0, step, 1:2, :] * pltpu.roll(sh, SSM_STATE, axis=1)
        d *= 2
        step += 1
    prev = jnp.where(row >= 1, pltpu.roll(st, 1, axis=0), 0.0)
    acc = acc + jnp.dot(prev.astype(BF16), v_ref[0], preferred_element_type=F32)
    for h in range(SSM_GROUP):
        y_ref[0, h] = acc[:, h * CHUNK:(h + 1) * CHUNK] + dsk_ref[0, h] * us[h]


def _ssm(u4, klag, wst, vout, pq, dsk, *, cpb):
    g, hh, nc, t = u4.shape
    blk = lambda shape: pl.BlockSpec((1,) + shape, lambda i: (i,) + (0,) * len(shape))
    return pl.pallas_call(
        functools.partial(_ssm_kernel, cpb=cpb),
        grid=(g,),
        in_specs=[blk((hh, nc, t)), blk(klag.shape[1:]),
                  pl.BlockSpec((1,) + klag.shape[1:], lambda i: (jnp.minimum(i + 1, g - 1), 0, 0, 0)),
                  blk(wst.shape[1:]), blk(vout.shape[1:]), blk(pq.shape[1:]), blk(dsk.shape[1:])],
        out_specs=blk((hh, nc, t)),
        out_shape=jax.ShapeDtypeStruct(u4.shape, F32),
        scratch_shapes=[pltpu.VMEM((hh * t, hh * t), BF16), pltpu.VMEM((hh * t, hh * t), BF16)],
        compiler_params=pltpu.CompilerParams(dimension_semantics=("arbitrary",),
                                             vmem_limit_bytes=VMEM_LIMIT),
        name="ssm",
    )(u4, klag, klag, wst, vout, pq, dsk)


def _ssm_params(lam_re, lam_im, log_dt, b_re, b_im, c_re, c_im, cpb):
    lr, li = lam_re.astype(F32), lam_im.astype(F32)
    dt = jnp.exp(log_dt.astype(F32))[:, None]
    mag = jnp.exp(lr * dt)
    ab_re = mag * jnp.cos(li * dt)
    ab_im = mag * jnp.sin(li * dt)
    den = lr * lr + li * li
    nr = ab_re - 1.0
    q_re = (nr * lr + ab_im * li) / den
    q_im = (ab_im * lr - nr * li) / den
    br, bi = b_re.astype(F32), b_im.astype(F32)
    bb_re = q_re[..., None] * br - q_im[..., None] * bi
    bb_im = q_re[..., None] * bi + q_im[..., None] * br
    cr, ci = c_re.astype(F32), c_im.astype(F32)

    pr, pi = jnp.ones((1,) + ab_re.shape, F32), jnp.zeros((1,) + ab_re.shape, F32)
    cur_r, cur_i = ab_re, ab_im
    while pr.shape[0] < CHUNK:
        pr, pi = (jnp.concatenate([pr, pr * cur_r - pi * cur_i], axis=0),
                  jnp.concatenate([pi, pr * cur_i + pi * cur_r], axis=0))
        cur_r, cur_i = cur_r * cur_r - cur_i * cur_i, 2.0 * cur_r * cur_i
    cp_re = cr[None] * pr[:, :, None, :] - ci[None] * pi[:, :, None, :]
    cp_im = cr[None] * pi[:, :, None, :] + ci[None] * pr[:, :, None, :]
    klag = (jnp.einsum('lghp,gpk->gkhl', cp_re, bb_re, precision=HIGHEST)
            - jnp.einsum('lghp,gpk->gkhl', cp_im, bb_im, precision=HIGHEST))
    rr, ri = pr[::-1], pi[::-1]
    w_re = rr[..., None] * bb_re[None] - ri[..., None] * bb_im[None]
    w_im = rr[..., None] * bb_im[None] + ri[..., None] * bb_re[None]
    wst = jnp.concatenate([w_re, w_im], axis=2).transpose(1, 3, 0, 2)
    wst = wst.reshape(N_GROUPS, SSM_GROUP * CHUNK, 2 * SSM_STATE).astype(BF16)
    p1r, p1i = pr * ab_re - pi * ab_im, pr * ab_im + pi * ab_re
    v_re = cr[None] * p1r[:, :, None, :] - ci[None] * p1i[:, :, None, :]
    v_im = cr[None] * p1i[:, :, None, :] + ci[None] * p1r[:, :, None, :]
    vout = jnp.concatenate([v_re, -v_im], axis=3).transpose(1, 3, 2, 0)
    vout = vout.reshape(N_GROUPS, 2 * SSM_STATE, SSM_GROUP * CHUNK).astype(BF16)
    tabs = []
    d = 1
    while d < cpb or not tabs:
        tabs.append(jnp.stack([jnp.concatenate([cur_r, cur_r], axis=1),
                               jnp.concatenate([-cur_i, cur_i], axis=1)], axis=1))
        cur_r, cur_i = cur_r * cur_r - cur_i * cur_i, 2.0 * cur_r * cur_i
        d *= 2
    pq = jnp.stack(tabs, axis=1)
    return klag, wst, vout, pq


def _mix_kernel(x_ref, yt_ref, nc_ref, mod_ref, wglu_ref, bglu_ref, ogs_ref, wot_ref, wob_ref, g2_ref,
                rw_ref, rb_ref, x1_ref, h2_ref, idx_ref, tw_ref):
    yt = jnp.concatenate([yt_ref[:, cc, :] for cc in range(yt_ref.shape[1])], axis=1)
    yg = 0.5 * yt * (1.0 + jnp.tanh(math.sqrt(2.0 / math.pi) * (yt + 0.044715 * (yt * yt * yt))))
    gate = jnp.dot(wglu_ref[...], yg.astype(BF16), preferred_element_type=F32) + bglu_ref[...]
    y2 = yg * jax.nn.sigmoid(gate)
    ms = jnp.mean(y2 * y2, axis=0, keepdims=True)
    ns = (y2 * lax.rsqrt(ms + RMS_EPS) * ogs_ref[...]).astype(BF16)
    o = (lax.dot_general(ns, wot_ref[...], (((0,), (0,)), ((), ())), preferred_element_type=F32)
         + jnp.dot(nc_ref[0], wob_ref[...], preferred_element_type=F32))
    x1 = x_ref[0] + mod_ref[0, 2:3, :] * o
    x1_ref[0] = x1
    h2 = _rms(x1) * g2_ref[...] * (1.0 + mod_ref[0, 4:5, :]) + mod_ref[0, 3:4, :]
    rpt = h2.shape[1] // LANES
    for j in range(rpt):
        h2_ref[pl.ds(j, h2.shape[0], stride=rpt), :] = h2[:, j * LANES:(j + 1) * LANES]
    nt_dims = (((1,), (1,)), ((), ()))
    h2_hi = h2.astype(BF16)
    h2_lo = (h2 - h2_hi.astype(F32)).astype(BF16)
    l_hi = lax.dot_general(rw_ref[...], h2_hi, nt_dims, preferred_element_type=F32)
    l_lo = lax.dot_general(rw_ref[:N_EXPERTS, :], h2_lo, nt_dims, preferred_element_type=F32)
    logits = l_hi[:N_EXPERTS, :] + (l_hi[N_EXPERTS:, :] + l_lo) + rb_ref[...]
    expert = lax.broadcasted_iota(jnp.int32, logits.shape, 0)
    vals, idxs = [], []
    for _ in range(TOP_K):
        m = jnp.max(logits, axis=0, keepdims=True)
        sel = jnp.min(jnp.where(logits == m, expert, N_EXPERTS), axis=0, keepdims=True)
        vals.append(m)
        idxs.append(sel)
        logits = jnp.where(expert == sel, -jnp.inf, logits)
    es = [jnp.exp(v - vals[0]) for v in vals]
    tot = es[0] + es[1] + es[2] + es[3]
    idx_ref[...] = jnp.concatenate(idxs, axis=0)
    tw_rows = jnp.concatenate([e / tot for e in es] + [jnp.zeros((LANES - TOP_K, h2.shape[0]), F32)], axis=0)
    tw_ref[...] = tw_rows.T[:, :TOP_K]


def _mix(x, yt, nconv, mod3, wglu_t, bglu, og_ssm, wo_top, wo_bot, norm2_g, router_w, router_b, *, tm):
    bsz, seq, d = x.shape
    nt = seq // tm
    n_tok = bsz * seq
    full = lambda shape: pl.BlockSpec(shape, lambda b, i: (0,) * len(shape))
    return pl.pallas_call(
        _mix_kernel,
        grid=(bsz, nt),
        in_specs=[pl.BlockSpec((1, tm, d), lambda b, i: (b, i, 0)),
                  pl.BlockSpec((D_SSM, tm // CHUNK, CHUNK), lambda b, i: (0, b * nt + i, 0)),
                  pl.BlockSpec((1, tm, D_CONV), lambda b, i: (b, i, 0)),
                  pl.BlockSpec((1, 6, d), lambda b, i: (b, 0, 0)),
                  full((D_SSM, D_SSM)), full((D_SSM, 1)), full((D_SSM, 1)),
                  full((D_SSM, d)), full((D_CONV, d)), full((1, d)),
                  full((2 * N_EXPERTS, d)), full((N_EXPERTS, 1))],
        out_specs=[pl.BlockSpec((1, tm, d), lambda b, i: (b, i, 0)),
                   pl.BlockSpec((tm * d // LANES, LANES), lambda b, i: (b * nt + i, 0)),
                   pl.BlockSpec((TOP_K, tm), lambda b, i: (0, b * nt + i)),
                   pl.BlockSpec((tm, TOP_K), lambda b, i: (b * nt + i, 0))],
        out_shape=[jax.ShapeDtypeStruct((bsz, seq, d), F32),
                   jax.ShapeDtypeStruct((n_tok * d // LANES, LANES), F32),
                   jax.ShapeDtypeStruct((TOP_K, n_tok), jnp.int32),
                   jax.ShapeDtypeStruct((n_tok, TOP_K), F32)],
        compiler_params=pltpu.CompilerParams(dimension_semantics=("arbitrary", "arbitrary"),
                                             vmem_limit_bytes=VMEM_LIMIT),
        name="mix",
    )(x, yt, nconv, mod3, wglu_t, bglu, og_ssm, wo_top, wo_bot, norm2_g, router_w, router_b)


def _moe_kernel(be_ref, nv_ref, inv_ref, h_ref, wg_ref, bg_ref, wu_ref, bu_ref, wd_ref, bd_ref,
                ys_ref, xbuf, ybuf, slots, wg_s, wu_s, wd_s, gsem, ssem, isem,
                *, tb, fc, n_blocks, n_tok):
    i = pl.program_id(0)
    b = i % 2
    nv = nv_ref[i]
    f = wg_s.shape[1]
    d = f
    rpt = d // LANES

    def ring(blk):
        return lax.rem(blk, SLOT_RING) * tb

    def gather_row(buf, base, r):
        tok = slots[base + r] & (n_tok - 1)
        return pltpu.make_async_copy(h_ref.at[pl.ds(pl.multiple_of(tok * rpt, rpt), rpt), :],
                                     xbuf.at[buf, pl.ds(pl.multiple_of(r * rpt, rpt), rpt), :], gsem.at[buf])

    def scatter_row(buf, base, r):
        return pltpu.make_async_copy(ybuf.at[buf, pl.ds(pl.multiple_of(r * rpt, rpt), rpt), :],
                                     ys_ref.at[pl.ds(pl.multiple_of(slots[base + r] * rpt, rpt), rpt), :],
                                     ssem.at[buf])

    def gather_all(buf):
        return pltpu.make_async_copy(h_ref.at[pl.ds(0, tb * rpt), :], xbuf.at[buf], gsem.at[buf])

    def scatter_all(buf):
        return pltpu.make_async_copy(ybuf.at[buf], ys_ref.at[pl.ds(0, tb * rpt), :], ssem.at[buf])

    def gather_one(buf):
        return pltpu.make_async_copy(h_ref.at[pl.ds(0, rpt), :], xbuf.at[buf, pl.ds(0, rpt), :], gsem.at[buf])

    def scatter_one(buf):
        return pltpu.make_async_copy(ybuf.at[buf, pl.ds(0, rpt), :], ys_ref.at[pl.ds(0, rpt), :], ssem.at[buf])

    def on_buffer(which, fn):
        for buf in (0, 1):
            @pl.when(which == buf)
            def _():
                fn(buf)

    def start_rows(n, row_copy, priority=0):
        @pl.when(n == tb)
        def _():
            for r in range(tb):
                row_copy(r).start(priority=priority)

        @pl.when(n < tb)
        def _():
            def body(r, carry):
                row_copy(r).start(priority=priority)
                return carry
            lax.fori_loop(0, n, body, 0)

    def wait_rows(n, all_copy, row_copy):
        @pl.when(n == tb)
        def _():
            all_copy.wait()

        @pl.when(n < tb)
        def _():
            def body(r, carry):
                row_copy.wait()
                return carry
            lax.fori_loop(0, n, body, 0)

    def slots_copy(blk):
        which = lax.rem(blk, SLOT_RING)
        return pltpu.make_async_copy(inv_ref.at[jnp.minimum(blk, n_blocks - 1), 0],
                                     slots.at[pl.ds(which * tb, tb)], isem.at[which])

    def nv_of(blk):
        return jnp.where((blk >= 0) & (blk < n_blocks), nv_ref[jnp.clip(blk, 0, n_blocks - 1)], 0)

    nv_next = nv_of(i + 1)

    @pl.when(i == 0)
    def _():
        xbuf[...] = jnp.zeros_like(xbuf)
        slots_copy(0).start()
        slots_copy(0).wait()
        start_rows(nv, lambda r: gather_row(0, 0, r), priority=ROW_DMA_PRIORITY)

        @pl.when(nv_next > 0)
        def _():
            slots_copy(1).start()

    def wait_inputs(buf):
        wait_rows(nv, gather_all(buf), gather_one(buf))
        wait_rows(nv_of(i - 2), scatter_all(buf), scatter_one(buf))

    on_buffer(b, wait_inputs)

    @pl.when(nv_of(i + 2) > 0)
    def _():
        slots_copy(i + 2).start()

    def prefetch_next(buf):
        slots_copy(i + 1).wait()
        base = ring(i + 1)
        start_rows(nv_next, lambda r: gather_row(buf, base, r), priority=ROW_DMA_PRIORITY)

    @pl.when(nv_next > 0)
    def _():
        on_buffer(1 - b, prefetch_next)

    @pl.when(nv > 0)
    def _():
        @pl.when((i == 0) | (be_ref[i] != be_ref[jnp.maximum(i - 1, 0)]))
        def _():
            wg_s[...] = wg_ref[0].astype(BF16)
            wu_s[...] = wu_ref[0].astype(BF16)
            wd_s[...] = wd_ref[0].astype(BF16)

        xb = jnp.concatenate([xbuf[b, pl.ds(j, tb, stride=rpt), :] for j in range(rpt)], axis=1).astype(BF16)
        y = None
        for j in range(f // fc):
            cols = slice(j * fc, (j + 1) * fc)
            gate = jnp.minimum(jnp.dot(xb, wg_s[:, cols], preferred_element_type=F32) + bg_ref[0, :, cols],
                               SWIGLU_LIMIT)
            up = jnp.clip(jnp.dot(xb, wu_s[:, cols], preferred_element_type=F32) + bu_ref[0, :, cols],
                          -SWIGLU_LIMIT, SWIGLU_LIMIT)
            act = ((up + 1.0) * (gate * jax.nn.sigmoid(SWIGLU_ALPHA * gate))).astype(BF16)
            part = jnp.dot(act, wd_s[cols, :], preferred_element_type=F32)
            y = part if y is None else y + part
        y = y + bd_ref[0]
        for j in range(rpt):
            ybuf[b, pl.ds(j, tb, stride=rpt), :] = y[:, j * LANES:(j + 1) * LANES]

    def send_outputs(buf):
        base = ring(i)
        start_rows(nv, lambda r: scatter_row(buf, base, r), priority=ROW_DMA_PRIORITY)

    on_buffer(b, send_outputs)

    @pl.when(i == n_blocks - 1)
    def _():
        on_buffer(1 - b, lambda buf: wait_rows(nv_of(i - 1), scatter_all(buf), scatter_one(buf)))
        on_buffer(b, lambda buf: wait_rows(nv, scatter_all(buf), scatter_one(buf)))


def _moe(block_e, block_nv, inv, h2t, w_gate, b_gate, w_up, b_up, w_down, b_down, *, tb, fc):
    e, d, f = w_gate.shape
    rpt = d // LANES
    n_tok = h2t.shape[0] // rpt
    n_blocks = block_nv.shape[0]
    assert n_tok & (n_tok - 1) == 0, "token count must be a power of two (slot -> token by masking)"
    assert d == f, "expert width equals model width in this block"
    wspec = lambda shape: pl.BlockSpec((1,) + shape, lambda i, be, nv: (be[i], 0, 0))
    return pl.pallas_call(
        functools.partial(_moe_kernel, tb=tb, fc=fc, n_blocks=n_blocks, n_tok=n_tok),
        grid_spec=pltpu.PrefetchScalarGridSpec(
            num_scalar_prefetch=2,
            grid=(n_blocks,),
            in_specs=[pl.BlockSpec(memory_space=pl.ANY), pl.BlockSpec(memory_space=pl.ANY),
                      wspec((d, f)), wspec((1, f)), wspec((d, f)), wspec((1, f)),
                      wspec((f, d)), wspec((1, d))],
            out_specs=pl.BlockSpec(memory_space=pl.ANY),
            scratch_shapes=[pltpu.VMEM((2, tb * rpt, LANES), F32), pltpu.VMEM((2, tb * rpt, LANES), F32),
                            pltpu.SMEM((SLOT_RING * tb,), jnp.int32),
                            pltpu.VMEM((d, f), BF16), pltpu.VMEM((d, f), BF16), pltpu.VMEM((f, d), BF16),
                            pltpu.SemaphoreType.DMA((2,)), pltpu.SemaphoreType.DMA((2,)),
                            pltpu.SemaphoreType.DMA((SLOT_RING,))]),
        out_shape=jax.ShapeDtypeStruct((TOP_K * n_tok * rpt, LANES), F32),
        compiler_params=pltpu.CompilerParams(dimension_semantics=("arbitrary",),
                                             vmem_limit_bytes=VMEM_LIMIT),
        name="moe",
    )(block_e, block_nv, inv, h2t, w_gate, b_gate.reshape(e, 1, f), w_up, b_up.reshape(e, 1, f),
      w_down, b_down.reshape(e, 1, d))


def _combine_kernel(ys_ref, x1_ref, tw_ref, mod_ref, fg_ref, o_ref):
    tw = tw_ref[...]
    tc, d = x1_ref.shape
    rpt = d // LANES

    def expert_rows(k):
        return jnp.concatenate([ys_ref[k, pl.ds(j, tc, stride=rpt), :] for j in range(rpt)], axis=1)

    moe = tw[:, 0:1] * expert_rows(0)
    for k in range(1, TOP_K):
        moe = moe + tw[:, k:k + 1] * expert_rows(k)
    xo = x1_ref[...] + mod_ref[0, 5:6, :] * moe
    o_ref[...] = _rms(xo) * fg_ref[...]


def _combine(ys4, x1, top_w, mod3, final_g, *, tc, seq):
    n_tok, d = x1.shape
    tiles_per_batch = seq // tc
    return pl.pallas_call(
        _combine_kernel,
        grid=(n_tok // tc,),
        in_specs=[pl.BlockSpec((TOP_K, tc * d // LANES, LANES), lambda i: (0, i, 0)),
                  pl.BlockSpec((tc, d), lambda i: (i, 0)),
                  pl.BlockSpec((tc, TOP_K), lambda i: (i, 0)),
                  pl.BlockSpec((1, 6, d), lambda i: (i // tiles_per_batch, 0, 0)),
                  pl.BlockSpec((1, d), lambda i: (0, 0))],
        out_specs=pl.BlockSpec((tc, d), lambda i: (i, 0)),
        out_shape=jax.ShapeDtypeStruct((n_tok, d), F32),
        compiler_params=pltpu.CompilerParams(dimension_semantics=("arbitrary",),
                                             vmem_limit_bytes=VMEM_LIMIT),
        name="combine",
    )(ys4, x1, top_w, mod3, final_g)


def _forward(x, c, ada_w, ada_b, norm1_g, w_in, lam_re, lam_im, log_dt, b_re, b_im, c_re, c_im, d_skip,
             w_glu, b_glu, conv_w, conv_b, ln_g, ln_b, out_norm_g, w_out, norm2_g, router_w, router_b,
             w_gate, b_gate, w_up, b_up, w_down, b_down, final_g, *, tm, rc, tb, fc, tc):
    bsz, seq, d = x.shape
    n_tok = bsz * seq
    cpb = seq // CHUNK
    n_chunks = n_tok // CHUNK
    row = lambda v: v.reshape(1, -1).astype(F32)

    mod3 = _mod(c, ada_w, ada_b).reshape(bsz, 6, d)

    wu_t = w_in[:, :D_SSM].T.astype(BF16)
    wvg = w_in[:, D_SSM:].astype(BF16)
    cw = jnp.zeros((CONV_HALO, D_CONV), F32).at[:CONV_WIDTH].set(conv_w.astype(F32))
    ut, nconv = _inproj(x, mod3, row(norm1_g), wu_t, wvg, cw, row(conv_b), row(ln_g), row(ln_b),
                        row(out_norm_g[D_SSM:]), tm=tm, rc=rc)

    klag, wst, vout, pq = _ssm_params(lam_re, lam_im, log_dt, b_re, b_im, c_re, c_im, cpb)
    dsk = jnp.broadcast_to(d_skip.astype(F32).reshape(N_GROUPS, SSM_GROUP, 1, 1),
                           (N_GROUPS, SSM_GROUP, 1, CHUNK))
    y4 = _ssm(ut.reshape(N_GROUPS, SSM_GROUP, n_chunks, CHUNK), klag, wst, vout, pq, dsk, cpb=cpb)
    yt = y4.reshape(D_SSM, n_chunks, CHUNK)

    rw = router_w.astype(F32)
    rw_hi = rw.astype(BF16)
    rw_split = jnp.concatenate([rw_hi, (rw - rw_hi.astype(F32)).astype(BF16)], axis=1).T
    x1, h2, top_idx, top_w = _mix(
        x, yt, nconv, mod3, w_glu.T.astype(BF16), b_glu.reshape(D_SSM, 1).astype(F32),
        out_norm_g[:D_SSM].reshape(D_SSM, 1).astype(F32), w_out[:D_SSM].astype(BF16),
        w_out[D_SSM:].astype(BF16), row(norm2_g), rw_split, router_b.reshape(N_EXPERTS, 1).astype(F32), tm=tm)

    n_assign = n_tok * TOP_K
    n_blocks = n_assign // tb + N_EXPERTS
    assert n_assign & (n_assign - 1) == 0 and (N_EXPERTS + 1) * n_assign < 2 ** 31
    experts = jnp.arange(N_EXPERTS, dtype=jnp.int32)
    counts = jnp.sum((top_idx[None] == experts[:, None, None]).astype(jnp.int32), axis=(1, 2))
    padded = (counts + tb - 1) // tb * tb
    pend = jnp.cumsum(padded)
    pstart = pend - padded
    blk_start = jnp.arange(n_blocks, dtype=jnp.int32) * tb
    block_e = jnp.minimum(jnp.sum((blk_start[:, None] >= pend[None, :]).astype(jnp.int32), axis=1), N_EXPERTS - 1)
    block_nv = jnp.clip(pstart[block_e] + counts[block_e] - blk_start, 0, tb).astype(jnp.int32)
    block_nv = jnp.where(blk_start < pend[-1], block_nv, 0)

    slot = jnp.arange(n_assign, dtype=jnp.int32).reshape(TOP_K, n_tok)
    pad_end = jnp.cumsum(padded - counts)
    pad_id = jnp.arange(n_blocks * tb - n_assign, dtype=jnp.int32)
    pad_e = jnp.sum((pad_id[:, None] >= pad_end[None, :]).astype(jnp.int32), axis=1)
    keys = jnp.concatenate([(top_idx * n_assign + slot).reshape(-1), pad_e * n_assign + (n_assign - 1)])
    inv = (lax.sort(keys) & (n_assign - 1)).reshape(n_blocks, 1, tb)

    ys4 = _moe(block_e, block_nv, inv, h2, w_gate, b_gate, w_up, b_up, w_down, b_down, tb=tb, fc=fc)
    out = _combine(ys4.reshape(TOP_K, n_tok * d // LANES, LANES), x1.reshape(n_tok, d), top_w, mod3,
                   row(final_g), tc=tc, seq=seq)
    return out.reshape(bsz, seq, d)


def kernel(x, c, ada_w, ada_b, norm1_g, w_in, lam_re, lam_im, log_dt, b_re, b_im, c_re, c_im, d_skip, w_glu, b_glu, conv_w, conv_b, ln_g, ln_b, out_norm_g, w_out, norm2_g, router_w, router_b, w_gate, b_gate, w_up, b_up, w_down, b_down, final_g):
    p = [a[0] for a in (ada_w, ada_b, norm1_g, w_in, lam_re, lam_im, log_dt, b_re, b_im, c_re, c_im, d_skip,
                        w_glu, b_glu, conv_w, conv_b, ln_g, ln_b, out_norm_g, w_out, norm2_g, router_w,
                        router_b, w_gate, b_gate, w_up, b_up, w_down, b_down)]
    return _forward(x, c, *p, final_g, **_tiles())
```
